```python
import jax, jax.numpy as jnp
from jax import lax
import numpy as np

D_MODEL = 1024
BATCH = 2
SEQ = 8192
DEPTH = 2

HEAD_DIM = 64
SB_HEADS = 8
MOBA_HEADS = 8
SB_WIDTH = SB_HEADS * HEAD_DIM
MOBA_WIDTH = MOBA_HEADS * HEAD_DIM
SB_Q_BLOCK = 128
MOBA_BLOCK = 256
MOBA_TOPK = 3
MOBA_Q_CHUNK = 128
ROPE_THETA = 500000.0
ROPE_DIM = HEAD_DIM // 4
D_FF = -(-8 * D_MODEL // (3 * 256)) * 256
DEEPNORM_ALPHA = (2 * DEPTH) ** 0.25
DEEPNORM_BETA = (8 * DEPTH) ** -0.25
LN_EPS = 1e-5
IN_COLS = 3 * SB_WIDTH + 3 * MOBA_WIDTH + 2 * D_MODEL

kernel_name = "hybrid_stickbreaking_moba_swiglu_deepnorm"


def layer_norm(x, g, b):
    xf = x.astype(jnp.float32)
    mu = jnp.mean(xf, axis=-1, keepdims=True)
    var = jnp.mean(jnp.square(xf - mu), axis=-1, keepdims=True)
    y = (xf - mu) * lax.rsqrt(var + LN_EPS) * g.astype(jnp.float32) + b.astype(jnp.float32)
    return y.astype(x.dtype)


def partial_rope(x, pos):
    half = ROPE_DIM // 2
    inv_freq = ROPE_THETA ** (-jnp.arange(0, ROPE_DIM, 2, dtype=jnp.float32) / ROPE_DIM)
    ang = pos.astype(jnp.float32)[:, None] * inv_freq[None, :]
    cos, sin = jnp.cos(ang), jnp.sin(ang)
    xf = x.astype(jnp.float32)
    x1, x2, rest = xf[..., :half], xf[..., half:ROPE_DIM], xf[..., ROPE_DIM:]
    out = jnp.concatenate([x1 * cos - x2 * sin, x2 * cos + x1 * sin, rest], axis=-1)
    return out.astype(x.dtype)


def stick_breaking_attention(q, k, v):
    B, H, S, dh = q.shape
    nblk = S // SB_Q_BLOCK
    scale = dh ** -0.5
    kpos = jnp.arange(S)
    qb = q.reshape(B, H, nblk, SB_Q_BLOCK, dh).transpose(2, 0, 1, 3, 4)

    def block(args):
        qi, i = args
        qpos = i * SB_Q_BLOCK + jnp.arange(SB_Q_BLOCK)
        z = jnp.einsum('bhqd,bhkd->bhqk', qi, k).astype(jnp.float32) * scale
        past = kpos[None, :] < qpos[:, None]
        log_keep = jnp.where(past, jax.nn.log_sigmoid(-z), 0.0)
        suffix = lax.cumsum(log_keep, axis=3, reverse=True) - log_keep
        w = jnp.where(past, jnp.exp(jax.nn.log_sigmoid(z) + suffix), 0.0)
        return jnp.einsum('bhqk,bhkd->bhqd', w.astype(v.dtype), v)

    out = lax.map(block, (qb, jnp.arange(nblk)))
    return out.transpose(1, 2, 0, 3, 4).reshape(B, H, S, dh)


def moba_attention(q, k, v):
    B, H, S, dh = q.shape
    nb = -(-S // MOBA_BLOCK)
    pad = nb * MOBA_BLOCK - S
    kp = jnp.pad(k, ((0, 0), (0, 0), (0, pad), (0, 0)))
    vp = jnp.pad(v, ((0, 0), (0, 0), (0, pad), (0, 0)))
    kb = kp.reshape(B, H, nb, MOBA_BLOCK, dh)
    vb = vp.reshape(B, H, nb, MOBA_BLOCK, dh)
    kmean = jnp.mean(kb.astype(jnp.float32), axis=3).astype(k.dtype)
    topk = min(MOBA_TOPK, nb)
    nchunk = S // MOBA_Q_CHUNK
    qc = q.reshape(B, H, nchunk, MOBA_Q_CHUNK, dh).transpose(2, 0, 1, 3, 4)
    bi = jnp.arange(B)[:, None, None, None]
    hi = jnp.arange(H)[None, :, None, None]
    blk_ids = jnp.arange(nb)
    scale = dh ** -0.5

    def chunk(args):
        qi, c = args
        qpos = c * MOBA_Q_CHUNK + jnp.arange(MOBA_Q_CHUNK)
        own = (c * MOBA_Q_CHUNK) // MOBA_BLOCK
        gate = jnp.einsum('bhqd,bhnd->bhqn', qi, kmean).astype(jnp.float32)
        gate = jnp.where(blk_ids < own, gate, -jnp.inf)
        _, idx = lax.top_k(gate, topk)
        valid = idx < own
        k_sel = kb[bi, hi, idx]
        v_sel = vb[bi, hi, idx]
        s_sel = jnp.einsum('bhqd,bhqnkd->bhqnk', qi, k_sel).astype(jnp.float32) * scale
        s_sel = jnp.where(valid[..., None], s_sel, -jnp.inf).reshape(B, H, MOBA_Q_CHUNK, topk * MOBA_BLOCK)
        k_own = lax.dynamic_slice_in_dim(kb, own, 1, axis=2)[:, :, 0]
        v_own = lax.dynamic_slice_in_dim(vb, own, 1, axis=2)[:, :, 0]
        s_own = jnp.einsum('bhqd,bhkd->bhqk', qi, k_own).astype(jnp.float32) * scale
        own_pos = own * MOBA_BLOCK + jnp.arange(MOBA_BLOCK)
        s_own = jnp.where(own_pos[None, :] <= qpos[:, None], s_own, -jnp.inf)
        p = jax.nn.softmax(jnp.concatenate([s_sel, s_own], axis=-1), axis=-1).astype(v.dtype)
        p_sel = p[..., :topk * MOBA_BLOCK].reshape(B, H, MOBA_Q_CHUNK, topk, MOBA_BLOCK)
        p_own = p[..., topk * MOBA_BLOCK:]
        return (jnp.einsum('bhqnk,bhqnkd->bhqd', p_sel, v_sel)
                + jnp.einsum('bhqk,bhkd->bhqd', p_own, v_own))

    out = lax.map(chunk, (qc, jnp.arange(nchunk)))
    return out.transpose(1, 2, 0, 3, 4).reshape(B, H, S, dh)


def heads(t, n_heads):
    B, S, _ = t.shape
    return t.reshape(B, S, n_heads, HEAD_DIM).transpose(0, 2, 1, 3)


def merge_heads(t):
    B, H, S, dh = t.shape
    return t.transpose(0, 2, 1, 3).reshape(B, S, H * dh)


def hybrid_mixer(x, w_in, w_branch_sb, w_branch_moba, w_out):
    S = x.shape[1]
    proj = x @ w_in
    cuts = np.cumsum([SB_WIDTH, SB_WIDTH, SB_WIDTH, MOBA_WIDTH, MOBA_WIDTH, MOBA_WIDTH, D_MODEL]).tolist()
    q_sb, k_sb, v_sb, q_mb, k_mb, v_mb, g_sb, g_mb = jnp.split(proj, cuts, axis=-1)
    pos = jnp.arange(S)
    o_sb = stick_breaking_attention(heads(q_sb, SB_HEADS), heads(k_sb, SB_HEADS), heads(v_sb, SB_HEADS))
    o_mb = moba_attention(partial_rope(heads(q_mb, MOBA_HEADS), pos),
                          partial_rope(heads(k_mb, MOBA_HEADS), pos),
                          heads(v_mb, MOBA_HEADS))
    branch_sb = merge_heads(o_sb) @ w_branch_sb
    branch_mb = merge_heads(o_mb) @ w_branch_moba
    merged = jax.nn.sigmoid(g_sb) * branch_sb + jax.nn.sigmoid(g_mb) * branch_mb
    return merged @ w_out


def swiglu(x, w_gate, w_up, w_down):
    return (jax.nn.silu(x @ w_gate) * (x @ w_up)) @ w_down


def setup_inputs(seed: int = 0) -> dict:
    key = jax.random.key(seed)
    ks = jax.random.split(key, 12)
    nrm = lambda k, shape, s: jax.random.normal(k, shape, jnp.float32) * s
    L = DEPTH
    return {
        "x": jax.random.normal(ks[0], (BATCH, SEQ, D_MODEL), jnp.float32),
        "w_in": nrm(ks[1], (L, D_MODEL, IN_COLS), D_MODEL ** -0.5),
        "w_branch_sb": nrm(ks[2], (L, SB_WIDTH, D_MODEL), SB_WIDTH ** -0.5),
        "w_branch_moba": nrm(ks[3], (L, MOBA_WIDTH, D_MODEL), MOBA_WIDTH ** -0.5),
        "w_out": nrm(ks[4], (L, D_MODEL, D_MODEL), D_MODEL ** -0.5 * DEEPNORM_BETA),
        "ln_mix_g": 1.0 + nrm(ks[5], (L, D_MODEL), 0.02),
        "ln_mix_b": nrm(ks[6], (L, D_MODEL), 0.02),
        "w_ffn_gate": nrm(ks[7], (L, D_MODEL, D_FF), D_MODEL ** -0.5),
        "w_ffn_up": nrm(ks[8], (L, D_MODEL, D_FF), D_MODEL ** -0.5),
        "w_ffn_down": nrm(ks[9], (L, D_FF, D_MODEL), D_FF ** -0.5 * DEEPNORM_BETA),
        "ln_ffn_g": 1.0 + nrm(ks[10], (L, D_MODEL), 0.02),
        "ln_ffn_b": nrm(ks[11], (L, D_MODEL), 0.02),
    }


def reference(x, w_in, w_branch_sb, w_branch_moba, w_out, ln_mix_g, ln_mix_b,
              w_ffn_gate, w_ffn_up, w_ffn_down, ln_ffn_g, ln_ffn_b):
    for l in range(DEPTH):
        mix = hybrid_mixer(x, w_in[l], w_branch_sb[l], w_branch_moba[l], w_out[l])
        x = layer_norm(DEEPNORM_ALPHA * x + mix, ln_mix_g[l], ln_mix_b[l])
        ffn = swiglu(x, w_ffn_gate[l], w_ffn_up[l], w_ffn_down[l])
        x = layer_norm(DEEPNORM_ALPHA * x + ffn, ln_ffn_g[l], ln_ffn_b[l])
    return x
```

```python
import functools

import jax
import jax.numpy as jnp
from jax import lax
from jax.experimental import pallas as pl
from jax.experimental.pallas import tpu as pltpu

HEAD_DIM = 64
SB_HEADS = 8
MOBA_HEADS = 8
SB_WIDTH = SB_HEADS * HEAD_DIM
MOBA_WIDTH = MOBA_HEADS * HEAD_DIM
QKV_COLS = 3 * SB_WIDTH + 3 * MOBA_WIDTH
MOBA_BLOCK = 256
MOBA_TOPK = 3
ROPE_THETA = 500000.0
ROPE_DIM = HEAD_DIM // 4
LN_EPS = 1e-5
SCALE = HEAD_DIM ** -0.5

LANES = 128
HEADS_PER_TILE = LANES // HEAD_DIM
VMEM_LIMIT = 56 * 1024 * 1024
PROJ_TN = 512
SB_TILE = 128
NEG_BIG = -1e30
SB_UNDERFLOW = -104.0

_NT = (((1,), (1,)), ((), ()))


def _bf16(a):
    return a.astype(jnp.bfloat16)


def _dot(a, b):
    return jnp.dot(a, b, preferred_element_type=jnp.float32)


def _dot_nt(a, b):
    return lax.dot_general(a, b, _NT, preferred_element_type=jnp.float32)


def _split_dot(a_f32, b_bf16):
    hi = _bf16(a_f32)
    lo = _bf16(a_f32 - hi.astype(jnp.float32))
    return _dot(hi, b_bf16) + _dot(lo, b_bf16)


def _layer_norm(y, g, b):
    mu = jnp.mean(y, axis=-1, keepdims=True)
    d = y - mu
    var = jnp.mean(d * d, axis=-1, keepdims=True)
    return d * lax.rsqrt(var + LN_EPS) * g + b


def _params(n_grid):
    return pltpu.CompilerParams(dimension_semantics=("arbitrary",) * n_grid,
                                vmem_limit_bytes=VMEM_LIMIT)


def _qkv_kernel(x_ref, w_ref, cos_ref, sina_ref, sinb_ref, o_ref, xb_ref, *, rope_lo, rope_hi):
    j = pl.program_id(1)

    @pl.when(j == 0)
    def _():
        xb_ref[...] = _bf16(x_ref[...])

    acc = _dot(xb_ref[...], w_ref[...])
    is_rope = (j >= rope_lo) & (j < rope_hi)

    @pl.when(is_rope)
    def _():
        reps = PROJ_TN // LANES
        wide = lambda r: jnp.concatenate([r[...]] * reps, axis=1)
        rot = (acc * wide(cos_ref)
               + pltpu.roll(acc, ROPE_DIM // 2, 1) * wide(sina_ref)
               + pltpu.roll(acc, PROJ_TN - ROPE_DIM // 2, 1) * wide(sinb_ref))
        o_ref[...] = _bf16(rot)

    @pl.when(jnp.logical_not(is_rope))
    def _():
        o_ref[...] = _bf16(acc)


def _qkv_proj(x2, w_qkv, cos_t, sina_t, sinb_t, seq, tm):
    t, d = x2.shape
    n = w_qkv.shape[1]
    s_tiles = seq // tm
    tab = pl.BlockSpec((tm, LANES), lambda i, j: (i % s_tiles, 0))
    rope_lo = 3 * SB_WIDTH // PROJ_TN
    rope_hi = (3 * SB_WIDTH + 2 * MOBA_WIDTH) // PROJ_TN
    return pl.pallas_call(
        functools.partial(_qkv_kernel, rope_lo=rope_lo, rope_hi=rope_hi),
        grid=(t // tm, n // PROJ_TN),
        in_specs=[pl.BlockSpec((tm, d), lambda i, j: (i, 0)),
                  pl.BlockSpec((d, PROJ_TN), lambda i, j: (0, j)),
                  tab, tab, tab],
        out_specs=pl.BlockSpec((tm, PROJ_TN), lambda i, j: (i, j)),
        out_shape=jax.ShapeDtypeStruct((t, n), jnp.bfloat16),
        scratch_shapes=[pltpu.VMEM((tm, d), jnp.bfloat16)],
        compiler_params=_params(2),
        name="qkv_proj",
    )(x2, w_qkv, cos_t, sina_t, sinb_t)


def _sb_kernel(q_ref, k_ref, v_ref, o_ref):
    i = pl.program_id(2)
    ts = SB_TILE
    q = q_ref[...]
    lane = lax.broadcasted_iota(jnp.int32, (ts, LANES), 1)
    row = lax.broadcasted_iota(jnp.int32, (ts, ts), 0)
    col = lax.broadcasted_iota(jnp.int32, (ts, ts), 1)
    past = col < row
    strict_upper = _bf16(row > col)

    def tile(j, qh, carry, acc, diagonal):
        start = pl.multiple_of(j * ts, ts)
        k = k_ref[pl.ds(start, ts), :]
        v = v_ref[pl.ds(start, ts), :]
        z = _dot_nt(qh, k) * SCALE
        softplus = jnp.maximum(z, 0.0) + jnp.log1p(jnp.exp(-jnp.abs(z)))
        log_keep = -softplus
        log_beta = z - softplus
        if diagonal:
            log_keep = jnp.where(past, log_keep, 0.0)
        suffix = _split_dot(log_keep, strict_upper)
        w = jnp.exp(log_beta + suffix + carry)
        if diagonal:
            w = jnp.where(past, w, 0.0)
        acc = acc + _dot(_bf16(w), v)
        carry = carry + suffix[:, :1] + log_keep[:, :1]
        return carry, acc

    outs = []
    for h in range(HEADS_PER_TILE):
        qh = jnp.where(lane // HEAD_DIM == h, q, jnp.zeros_like(q))
        carry0 = jnp.zeros((ts, 1), jnp.float32)
        acc0 = jnp.zeros((ts, LANES), jnp.float32)
        carry1, acc1 = tile(i, qh, carry0, acc0, True)

        def cond(state):
            j, bound, _, _ = state
            return (j >= 0) & (bound > SB_UNDERFLOW)

        def body(state, qh=qh):
            j, _, carry, acc = state
            carry, acc = tile(j, qh, carry, acc, False)
            return j - 1, jnp.max(carry), carry, acc

        _, _, _, acc = lax.while_loop(cond, body, (i - 1, jnp.max(carry1), carry1, acc1))
        outs.append(acc)

    o_ref[...] = _bf16(jnp.where(lane < HEAD_DIM, outs[0], outs[1]))


def _sb_attention(qkv, batch, seq):
    t = qkv.shape[0]
    nq = seq // SB_TILE
    hp = SB_WIDTH // LANES
    return pl.pallas_call(
        _sb_kernel,
        grid=(batch, hp, nq),
        in_specs=[pl.BlockSpec((SB_TILE, LANES), lambda b, p, i: (b * nq + i, p)),
                  pl.BlockSpec((seq, LANES), lambda b, p, i: (b, hp + p)),
                  pl.BlockSpec((seq, LANES), lambda b, p, i: (b, 2 * hp + p))],
        out_specs=pl.BlockSpec((SB_TILE, LANES), lambda b, p, i: (b * nq + i, p)),
        out_shape=jax.ShapeDtypeStruct((t, SB_WIDTH), jnp.bfloat16),
        compiler_params=_params(3),
        name="sb_attention",
    )(qkv, qkv, qkv)


def _moba_kernel(q_ref, k_ref, v_ref, o_ref, kmean_ref, sel_ref, m_ref, l_ref, acc_ref, *, n_blocks):
    i = pl.program_id(2)
    bs = MOBA_BLOCK

    @pl.when(i == 0)
    def _():
        kmean_ref[...] = jnp.zeros_like(kmean_ref)
        for n in range(n_blocks):
            blk = k_ref[n * bs:(n + 1) * bs, :].astype(jnp.float32)
            kmean_ref[n:n + 1, :] = jnp.sum(blk, axis=0, keepdims=True) * (1.0 / bs)

    q = q_ref[...]
    lane = lax.broadcasted_iota(jnp.int32, (bs, LANES), 1)
    row = lax.broadcasted_iota(jnp.int32, (bs, bs), 0)
    col = lax.broadcasted_iota(jnp.int32, (bs, bs), 1)
    kmean = kmean_ref[...]
    km_hi = _bf16(kmean)
    km_lo = _bf16(kmean - km_hi.astype(jnp.float32))
    own_start = pl.multiple_of(i * bs, bs)

    for h in range(HEADS_PER_TILE):
        qh = jnp.where(lane // HEAD_DIM == h, q, jnp.zeros_like(q))

        gate = _dot_nt(qh, km_hi) + _dot_nt(qh, km_lo)
        gate = jnp.where(lane < i, gate, -jnp.inf)
        sel = jnp.zeros((bs, LANES), jnp.float32)
        for _ in range(MOBA_TOPK):
            best = jnp.max(gate, axis=1, keepdims=True)
            first = jnp.min(jnp.where(gate == best, lane, LANES), axis=1, keepdims=True)
            pick = (lane == first) & (best > -jnp.inf)
            sel = jnp.where(pick, 1.0, sel)
            gate = jnp.where(pick, -jnp.inf, gate)
        sel_ref[h] = sel

        z = _dot_nt(qh, k_ref[pl.ds(own_start, bs), :]) * SCALE
        s = jnp.where(col <= row, z, NEG_BIG)
        m = jnp.max(s, axis=1, keepdims=True)
        p = jnp.exp(s - m)
        m_ref[h] = m
        l_ref[h] = jnp.sum(p, axis=1, keepdims=True)
        acc_ref[h] = _dot(_bf16(p), v_ref[pl.ds(own_start, bs), :])

        def past_block(n, carry, h=h, qh=qh):
            chosen = jnp.sum(jnp.where(lane == n, sel_ref[h], 0.0), axis=1, keepdims=True)

            @pl.when(jnp.max(chosen) > 0.0)
            def _():
                start = pl.multiple_of(n * bs, bs)
                z = _dot_nt(qh, k_ref[pl.ds(start, bs), :]) * SCALE
                s = jnp.where(chosen > 0.0, z, NEG_BIG)
                m_old = m_ref[h]
                m_new = jnp.maximum(m_old, jnp.max(s, axis=1, keepdims=True))
                p = jnp.exp(s - m_new)
                alpha = jnp.exp(m_old - m_new)
                m_ref[h] = m_new
                l_ref[h] = alpha * l_ref[h] + jnp.sum(p, axis=1, keepdims=True)
                acc_ref[h] = alpha * acc_ref[h] + _dot(_bf16(p), v_ref[pl.ds(start, bs), :])

            return carry

        lax.fori_loop(0, i, past_block, 0)

    out = [acc_ref[h] / l_ref[h] for h in range(HEADS_PER_TILE)]
    o_ref[...] = _bf16(jnp.where(lane < HEAD_DIM, out[0], out[1]))


def _moba_attention(qkv, batch, seq):
    t = qkv.shape[0]
    nq = seq // MOBA_BLOCK
    assert nq <= LANES, "block gates are laid out one block per lane"
    hp = MOBA_WIDTH // LANES
    base = 3 * SB_WIDTH // LANES
    hpt = HEADS_PER_TILE
    return pl.pallas_call(
        functools.partial(_moba_kernel, n_blocks=nq),
        grid=(batch, hp, nq),
        in_specs=[pl.BlockSpec((MOBA_BLOCK, LANES), lambda b, p, i: (b * nq + i, base + p)),
                  pl.BlockSpec((seq, LANES), lambda b, p, i: (b, base + hp + p)),
                  pl.BlockSpec((seq, LANES), lambda b, p, i: (b, base + 2 * hp + p))],
        out_specs=pl.BlockSpec((MOBA_BLOCK, LANES), lambda b, p, i: (b * nq + i, p)),
        out_shape=jax.ShapeDtypeStruct((t, MOBA_WIDTH), jnp.bfloat16),
        scratch_shapes=[pltpu.VMEM((LANES, LANES), jnp.float32),
                        pltpu.VMEM((hpt, MOBA_BLOCK, LANES), jnp.float32),
                        pltpu.VMEM((hpt, MOBA_BLOCK, 1), jnp.float32),
                        pltpu.VMEM((hpt, MOBA_BLOCK, 1), jnp.float32),
                        pltpu.VMEM((hpt, MOBA_BLOCK, LANES), jnp.float32)],
        compiler_params=_params(3),
        name="moba_attention",
    )(qkv, qkv, qkv)


def _mix_kernel(x_ref, osb_ref, omb_ref, wg_ref, wbs_ref, wbm_ref, wo_ref, g_ref, b_ref, o_ref, *, alpha):
    x = x_ref[...]
    d = x.shape[1]
    gates = _dot(_bf16(x), wg_ref[...])
    branch_sb = _dot(osb_ref[...], wbs_ref[...])
    branch_mb = _dot(omb_ref[...], wbm_ref[...])
    merged = jax.nn.sigmoid(gates[:, :d]) * branch_sb + jax.nn.sigmoid(gates[:, d:]) * branch_mb
    mix = _dot(_bf16(merged), wo_ref[...])
    o_ref[...] = _layer_norm(alpha * x + mix, g_ref[...], b_ref[...])


def _mix_block(x2, o_sb, o_mb, w_gate, w_bsb, w_bmb, w_out, ln_g, ln_b, alpha, tm):
    t, d = x2.shape
    rows = lambda w: pl.BlockSpec((tm, w), lambda i: (i, 0))
    whole = lambda a: pl.BlockSpec(a.shape, lambda i: (0, 0), pipeline_mode=pl.Buffered(1))
    return pl.pallas_call(
        functools.partial(_mix_kernel, alpha=alpha),
        grid=(t // tm,),
        in_specs=[rows(d), rows(SB_WIDTH), rows(MOBA_WIDTH),
                  whole(w_gate), whole(w_bsb), whole(w_bmb), whole(w_out), whole(ln_g), whole(ln_b)],
        out_specs=rows(d),
        out_shape=jax.ShapeDtypeStruct((t, d), jnp.float32),
        compiler_params=_params(1),
        name="mix_ln",
    )(x2, o_sb, o_mb, w_gate, w_bsb, w_bmb, w_out, ln_g, ln_b)


def _ffn_kernel(x_ref, wg_ref, wu_ref, wd_ref, g_ref, b_ref, o_ref, *, alpha, chunk):
    x = x_ref[...]
    xb = _bf16(x)
    acc = alpha * x
    for c in range(0, wg_ref.shape[1], chunk):
        gate = _dot(xb, wg_ref[:, c:c + chunk])
        up = _dot(xb, wu_ref[:, c:c + chunk])
        acc = acc + _dot(_bf16(jax.nn.silu(gate) * up), wd_ref[c:c + chunk, :])
    o_ref[...] = _layer_norm(acc, g_ref[...], b_ref[...])


def _ffn_chunk(d_ff):
    best = LANES
    for c in range(LANES, 1536 + 1, LANES):
        if d_ff % c == 0:
            best = c
    return best


def _ffn_block(x2, w_gate, w_up, w_down, ln_g, ln_b, alpha, tm):
    t, d = x2.shape
    rows = pl.BlockSpec((tm, d), lambda i: (i, 0))
    whole = lambda a: pl.BlockSpec(a.shape, lambda i: (0, 0), pipeline_mode=pl.Buffered(1))
    return pl.pallas_call(
        functools.partial(_ffn_kernel, alpha=alpha, chunk=_ffn_chunk(w_gate.shape[1])),
        grid=(t // tm,),
        in_specs=[rows, whole(w_gate), whole(w_up), whole(w_down), whole(ln_g), whole(ln_b)],
        out_specs=rows,
        out_shape=jax.ShapeDtypeStruct((t, d), jnp.float32),
        compiler_params=_params(1),
        name="ffn_ln",
    )(x2, w_gate, w_up, w_down, ln_g, ln_b)


def _rope_tables(seq):
    half = ROPE_DIM // 2
    inv_freq = ROPE_THETA ** (-jnp.arange(0, ROPE_DIM, 2, dtype=jnp.float32) / ROPE_DIM)
    ang = jnp.arange(seq).astype(jnp.float32)[:, None] * inv_freq[None, :]
    cos, sin = jnp.cos(ang), jnp.sin(ang)
    rest = HEAD_DIM - ROPE_DIM
    ones = jnp.ones((seq, rest), jnp.float32)
    zeros = jnp.zeros((seq, rest), jnp.float32)
    zh = jnp.zeros((seq, half), jnp.float32)
    per_head = lambda parts: jnp.concatenate(parts * HEADS_PER_TILE, axis=1)
    return (per_head([cos, cos, ones]), per_head([zh, sin, zeros]), per_head([-sin, zh, zeros]))


def kernel(x, w_in, w_branch_sb, w_branch_moba, w_out, ln_mix_g, ln_mix_b,
           w_ffn_gate, w_ffn_up, w_ffn_down, ln_ffn_g, ln_ffn_b):
    batch, seq, d = x.shape
    depth = w_in.shape[0]
    assert w_in.shape[2] == QKV_COLS + 2 * d
    assert seq % MOBA_BLOCK == 0 and seq % SB_TILE == 0
    alpha = (2 * depth) ** 0.25
    tm = min(512, seq)
    cos_t, sina_t, sinb_t = _rope_tables(seq)
    x2 = x.reshape(batch * seq, d)
    for l in range(depth):
        w_l = _bf16(w_in[l])
        qkv = _qkv_proj(x2, w_l[:, :QKV_COLS], cos_t, sina_t, sinb_t, seq, tm)
        o_sb = _sb_attention(qkv, batch, seq)
        o_mb = _moba_attention(qkv, batch, seq)
        x2 = _mix_block(x2, o_sb, o_mb, w_l[:, QKV_COLS:], _bf16(w_branch_sb[l]), _bf16(w_branch_moba[l]),
                        _bf16(w_out[l]), ln_mix_g[l][None, :], ln_mix_b[l][None, :], alpha, tm)
        x2 = _ffn_block(x2, _bf16(w_ffn_gate[l]), _bf16(w_ffn_up[l]), _bf16(w_ffn_down[l]),
                        ln_ffn_g[l][None, :], ln_ffn_b[l][None, :], alpha, tm)
    return x2.reshape(batch, seq, d)
```

```python
import functools

import jax
import jax.numpy as jnp
from jax import lax
from jax.experimental import pallas as pl
from jax.experimental.pallas import tpu as pltpu

HEAD_DIM = 64
SB_HEADS = 8
MOBA_HEADS = 8
SB_WIDTH = SB_HEADS * HEAD_DIM
MOBA_WIDTH = MOBA_HEADS * HEAD_DIM
QKV_COLS = 3 * SB_WIDTH + 3 * MOBA_WIDTH
MOBA_BLOCK = 256
MOBA_TOPK = 3
ROPE_THETA = 500000.0
ROPE_DIM = HEAD_DIM // 4
LN_EPS = 1e-5
SCALE = HEAD_DIM ** -0.5

LANES = 128
HEADS_PER_TILE = LANES // HEAD_DIM
VMEM_LIMIT = 56 * 1024 * 1024
PROJ_TN = 512
SB_TILE = 256
MOBA_GROUP = 4
NEG_BIG = -1e30
SB_UNDERFLOW = -104.0

_NT = (((1,), (1,)), ((), ()))


def _bf16(a):
    return a.astype(jnp.bfloat16)


def _dot(a, b):
    return jnp.dot(a, b, preferred_element_type=jnp.float32)


def _dot_nt(a, b):
    return lax.dot_general(a, b, _NT, preferred_element_type=jnp.float32)


def _split_dot(a_f32, b_bf16):
    hi = _bf16(a_f32)
    lo = _bf16(a_f32 - hi.astype(jnp.float32))
    return _dot(hi, b_bf16) + _dot(lo, b_bf16)


def _layer_norm(y, g, b):
    mu = jnp.mean(y, axis=-1, keepdims=True)
    d = y - mu
    var = jnp.mean(d * d, axis=-1, keepdims=True)
    return d * lax.rsqrt(var + LN_EPS) * g + b


def _params(n_grid):
    return pltpu.CompilerParams(dimension_semantics=("arbitrary",) * n_grid,
                                vmem_limit_bytes=VMEM_LIMIT)


def _qkv_kernel(x_ref, w_ref, cos_ref, sina_ref, sinb_ref, o_ref, xb_ref, *, rope_lo, rope_hi):
    j = pl.program_id(1)

    @pl.when(j == 0)
    def _():
        xb_ref[...] = _bf16(x_ref[...])

    acc = _dot(xb_ref[...], w_ref[...])
    is_rope = (j >= rope_lo) & (j < rope_hi)

    @pl.when(is_rope)
    def _():
        reps = PROJ_TN // LANES
        wide = lambda r: jnp.concatenate([r[...]] * reps, axis=1)
        rot = (acc * wide(cos_ref)
               + pltpu.roll(acc, ROPE_DIM // 2, 1) * wide(sina_ref)
               + pltpu.roll(acc, PROJ_TN - ROPE_DIM // 2, 1) * wide(sinb_ref))
        o_ref[...] = _bf16(rot)

    @pl.when(jnp.logical_not(is_rope))
    def _():
        o_ref[...] = _bf16(acc)


def _qkv_proj(x2, w_qkv, cos_t, sina_t, sinb_t, seq, tm):
    t, d = x2.shape
    n = w_qkv.shape[1]
    s_tiles = seq // tm
    tab = pl.BlockSpec((tm, LANES), lambda i, j: (i % s_tiles, 0))
    rope_lo = 3 * SB_WIDTH // PROJ_TN
    rope_hi = (3 * SB_WIDTH + 2 * MOBA_WIDTH) // PROJ_TN
    return pl.pallas_call(
        functools.partial(_qkv_kernel, rope_lo=rope_lo, rope_hi=rope_hi),
        grid=(t // tm, n // PROJ_TN),
        in_specs=[pl.BlockSpec((tm, d), lambda i, j: (i, 0)),
                  pl.BlockSpec((d, PROJ_TN), lambda i, j: (0, j)),
                  tab, tab, tab],
        out_specs=pl.BlockSpec((tm, PROJ_TN), lambda i, j: (i, j)),
        out_shape=jax.ShapeDtypeStruct((t, n), jnp.bfloat16),
        scratch_shapes=[pltpu.VMEM((tm, d), jnp.bfloat16)],
        compiler_params=_params(2),
        name="qkv_proj",
    )(x2, w_qkv, cos_t, sina_t, sinb_t)


def _sb_kernel(q_ref, k_ref, v_ref, o_ref):
    i = pl.program_id(2)
    ts = SB_TILE
    q = q_ref[...]
    lane = lax.broadcasted_iota(jnp.int32, (ts, LANES), 1)
    row = lax.broadcasted_iota(jnp.int32, (ts, ts), 0)
    col = lax.broadcasted_iota(jnp.int32, (ts, ts), 1)
    past = col < row
    strict_upper = _bf16(row > col)
    qs = [jnp.where(lane // HEAD_DIM == h, q, jnp.zeros_like(q)) * SCALE for h in range(HEADS_PER_TILE)]

    def tile(j, state, diagonal):
        start = pl.multiple_of(j * ts, ts)
        k = k_ref[pl.ds(start, ts), :]
        v = v_ref[pl.ds(start, ts), :]
        new = []
        for h in range(HEADS_PER_TILE):
            carry, acc = state[h]
            z = _dot_nt(qs[h], k)
            softplus = jnp.maximum(z, 0.0) + jnp.log1p(jnp.exp(-jnp.abs(z)))
            log_keep = -softplus
            log_beta = z - softplus
            if diagonal:
                log_keep = jnp.where(past, log_keep, 0.0)
            suffix = _split_dot(log_keep, strict_upper)
            w = jnp.exp(log_beta + suffix + carry)
            if diagonal:
                w = jnp.where(past, w, 0.0)
            new.append((carry + suffix[:, :1] + log_keep[:, :1], acc + _dot(_bf16(w), v)))
        return tuple(new)

    def bound(state):
        return jnp.max(jnp.maximum(state[0][0], state[1][0]))

    zero = (jnp.zeros((ts, 1), jnp.float32), jnp.zeros((ts, LANES), jnp.float32))
    first = tile(i, (zero,) * HEADS_PER_TILE, True)

    def cond(loop):
        j, worst, _ = loop
        return (j >= 0) & (worst > SB_UNDERFLOW)

    def body(loop):
        j, _, state = loop
        state = tile(j, state, False)
        return j - 1, bound(state), state

    _, _, state = lax.while_loop(cond, body, (i - 1, bound(first), first))
    o_ref[...] = _bf16(jnp.where(lane < HEAD_DIM, state[0][1], state[1][1]))


def _sb_attention(qkv, batch, seq):
    t = qkv.shape[0]
    nq = seq // SB_TILE
    hp = SB_WIDTH // LANES
    return pl.pallas_call(
        _sb_kernel,
        grid=(batch, hp, nq),
        in_specs=[pl.BlockSpec((SB_TILE, LANES), lambda b, p, i: (b * nq + i, p)),
                  pl.BlockSpec((seq, LANES), lambda b, p, i: (b, hp + p)),
                  pl.BlockSpec((seq, LANES), lambda b, p, i: (b, 2 * hp + p))],
        out_specs=pl.BlockSpec((SB_TILE, LANES), lambda b, p, i: (b * nq + i, p)),
        out_shape=jax.ShapeDtypeStruct((t, SB_WIDTH), jnp.bfloat16),
        compiler_params=_params(3),
        name="sb_attention",
    )(qkv, qkv, qkv)


def _moba_kernel(q_ref, k_ref, v_ref, o_ref, kmean_ref, vt_ref, sel_ref, *, n_blocks):
    i = pl.program_id(2)
    bs = MOBA_BLOCK
    hd = HEAD_DIM

    @pl.when(i == 0)
    def _():
        for n in range(n_blocks):
            kmean_ref[n:n + 1, :] = jnp.sum(k_ref[n * bs:(n + 1) * bs, :].astype(jnp.float32),
                                            axis=0, keepdims=True) * (1.0 / bs)
            vt_ref[n] = _bf16(v_ref[n * bs:(n + 1) * bs, :].astype(jnp.float32).T)

    q = q_ref[...]
    lane = lax.broadcasted_iota(jnp.int32, (bs, LANES), 1)
    key = lax.broadcasted_iota(jnp.int32, (bs, bs), 0)
    qry = lax.broadcasted_iota(jnp.int32, (bs, bs), 1)
    blk = lax.broadcasted_iota(jnp.int32, (n_blocks, bs), 0)
    kmean = kmean_ref[...]
    km_hi = _bf16(kmean)
    km_lo = _bf16(kmean - km_hi.astype(jnp.float32))
    own_start = pl.multiple_of(i * bs, bs)
    k_own = k_ref[pl.ds(own_start, bs), :]
    vt_own = vt_ref[i]

    heads = []
    for h in range(HEADS_PER_TILE):
        qh = jnp.where(lane // hd == h, q, jnp.zeros_like(q))
        qs = qh * SCALE

        gate = _dot_nt(km_hi, qh) + _dot_nt(km_lo, qh)
        gate = jnp.where(blk < i, gate, -jnp.inf)
        sel = jnp.zeros((n_blocks, bs), jnp.float32)
        for _ in range(MOBA_TOPK):
            best = jnp.max(gate, axis=0, keepdims=True)
            first = jnp.min(jnp.where(gate == best, blk, n_blocks), axis=0, keepdims=True)
            pick = (blk == first) & (best > -jnp.inf)
            sel = jnp.where(pick, 1.0, sel)
            gate = jnp.where(pick, -jnp.inf, gate)
        sel_ref[h] = sel

        s = jnp.where(key <= qry, _dot_nt(k_own, qs), NEG_BIG)
        m = jnp.max(s, axis=0, keepdims=True)
        p = jnp.exp(s - m)
        heads.append((qs, m, jnp.sum(p, axis=0, keepdims=True),
                      _dot(vt_own[h * hd:(h + 1) * hd, :], _bf16(p))))

    qs_all = [hh[0] for hh in heads]

    grp = MOBA_GROUP

    def past_group(g, state):
        first_blk = g * grp
        k = k_ref[pl.ds(pl.multiple_of(first_blk * bs, grp * bs), grp * bs), :]
        new = []
        for h in range(HEADS_PER_TILE):
            m_old, l_old, acc_old = state[h]
            z = _dot_nt(k, qs_all[h])
            s = [jnp.where(sel_ref[h, pl.ds(first_blk + c, 1), :] > 0.0, z[c * bs:(c + 1) * bs], NEG_BIG)
                 for c in range(grp)]
            m_new = m_old
            for sc in s:
                m_new = jnp.maximum(m_new, jnp.max(sc, axis=0, keepdims=True))
            alpha = jnp.exp(m_old - m_new)
            l_new = alpha * l_old
            acc_new = alpha * acc_old
            for c, sc in enumerate(s):
                p = jnp.exp(sc - m_new)
                l_new = l_new + jnp.sum(p, axis=0, keepdims=True)
                acc_new = acc_new + _dot(vt_ref[first_blk + c, h * hd:(h + 1) * hd, :], _bf16(p))
            new.append((m_new, l_new, acc_new))
        return tuple(new)

    state = lax.fori_loop(0, (i + grp - 1) // grp, past_group, tuple(hh[1:] for hh in heads))
    out_t = jnp.concatenate([acc / l for (_, l, acc) in state], axis=0)
    o_ref[...] = _bf16(out_t.T)


def _moba_attention(qkv, batch, seq):
    t = qkv.shape[0]
    nq = seq // MOBA_BLOCK
    assert nq % MOBA_GROUP == 0, "the grouped block walk reads whole groups"
    hp = MOBA_WIDTH // LANES
    base = 3 * SB_WIDTH // LANES
    return pl.pallas_call(
        functools.partial(_moba_kernel, n_blocks=nq),
        grid=(batch, hp, nq),
        in_specs=[pl.BlockSpec((MOBA_BLOCK, LANES), lambda b, p, i: (b * nq + i, base + p)),
                  pl.BlockSpec((seq, LANES), lambda b, p, i: (b, base + hp + p)),
                  pl.BlockSpec((seq, LANES), lambda b, p, i: (b, base + 2 * hp + p))],
        out_specs=pl.BlockSpec((MOBA_BLOCK, LANES), lambda b, p, i: (b * nq + i, p)),
        out_shape=jax.ShapeDtypeStruct((t, MOBA_WIDTH), jnp.bfloat16),
        scratch_shapes=[pltpu.VMEM((nq, LANES), jnp.float32),
                        pltpu.VMEM((nq, LANES, MOBA_BLOCK), jnp.bfloat16),
                        pltpu.VMEM((HEADS_PER_TILE, nq, MOBA_BLOCK), jnp.float32)],
        compiler_params=_params(3),
        name="moba_attention",
    )(qkv, qkv, qkv)


def _mix_kernel(x_ref, osb_ref, omb_ref, wg_ref, wbs_ref, wbm_ref, wo_ref, g_ref, b_ref, o_ref, *, alpha):
    x = x_ref[...]
    d = x.shape[1]
    gates = _dot(_bf16(x), wg_ref[...])
    branch_sb = _dot(osb_ref[...], wbs_ref[...])
    branch_mb = _dot(omb_ref[...], wbm_ref[...])
    merged = jax.nn.sigmoid(gates[:, :d]) * branch_sb + jax.nn.sigmoid(gates[:, d:]) * branch_mb
    mix = _dot(_bf16(merged), wo_ref[...])
    o_ref[...] = _layer_norm(alpha * x + mix, g_ref[...], b_ref[...])


def _mix_block(x2, o_sb, o_mb, w_gate, w_bsb, w_bmb, w_out, ln_g, ln_b, alpha, tm):
    t, d = x2.shape
    rows = lambda w: pl.BlockSpec((tm, w), lambda i: (i, 0))
    whole = lambda a: pl.BlockSpec(a.shape, lambda i: (0, 0), pipeline_mode=pl.Buffered(1))
    return pl.pallas_call(
        functools.partial(_mix_kernel, alpha=alpha),
        grid=(t // tm,),
        in_specs=[rows(d), rows(SB_WIDTH), rows(MOBA_WIDTH),
                  whole(w_gate), whole(w_bsb), whole(w_bmb), whole(w_out), whole(ln_g), whole(ln_b)],
        out_specs=rows(d),
        out_shape=jax.ShapeDtypeStruct((t, d), jnp.float32),
        compiler_params=_params(1),
        name="mix_ln",
    )(x2, o_sb, o_mb, w_gate, w_bsb, w_bmb, w_out, ln_g, ln_b)


def _ffn_kernel(x_ref, wg_ref, wu_ref, wd_ref, g_ref, b_ref, o_ref, *, alpha, chunk):
    x = x_ref[...]
    xb = _bf16(x)
    acc = alpha * x
    for c in range(0, wg_ref.shape[1], chunk):
        gate = _dot(xb, wg_ref[:, c:c + chunk])
        up = _dot(xb, wu_ref[:, c:c + chunk])
        acc = acc + _dot(_bf16(jax.nn.silu(gate) * up), wd_ref[c:c + chunk, :])
    o_ref[...] = _layer_norm(acc, g_ref[...], b_ref[...])


def _ffn_chunk(d_ff):
    best = LANES
    for c in range(LANES, 1536 + 1, LANES):
        if d_ff % c == 0:
            best = c
    return best


def _ffn_block(x2, w_gate, w_up, w_down, ln_g, ln_b, alpha, tm):
    t, d = x2.shape
    rows = pl.BlockSpec((tm, d), lambda i: (i, 0))
    whole = lambda a: pl.BlockSpec(a.shape, lambda i: (0, 0), pipeline_mode=pl.Buffered(1))
    return pl.pallas_call(
        functools.partial(_ffn_kernel, alpha=alpha, chunk=_ffn_chunk(w_gate.shape[1])),
        grid=(t // tm,),
        in_specs=[rows, whole(w_gate), whole(w_up), whole(w_down), whole(ln_g), whole(ln_b)],
        out_specs=rows,
        out_shape=jax.ShapeDtypeStruct((t, d), jnp.float32),
        compiler_params=_params(1),
        name="ffn_ln",
    )(x2, w_gate, w_up, w_down, ln_g, ln_b)


def _rope_tables(seq):
    half = ROPE_DIM // 2
    inv_freq = ROPE_THETA ** (-jnp.arange(0, ROPE_DIM, 2, dtype=jnp.float32) / ROPE_DIM)
    ang = jnp.arange(seq).astype(jnp.float32)[:, None] * inv_freq[None, :]
    cos, sin = jnp.cos(ang), jnp.sin(ang)
    rest = HEAD_DIM - ROPE_DIM
    ones = jnp.ones((seq, rest), jnp.float32)
    zeros = jnp.zeros((seq, rest), jnp.float32)
    zh = jnp.zeros((seq, half), jnp.float32)
    per_head = lambda parts: jnp.concatenate(parts * HEADS_PER_TILE, axis=1)
    return (per_head([cos, cos, ones]), per_head([zh, sin, zeros]), per_head([-sin, zh, zeros]))


def kernel(x, w_in, w_branch_sb, w_branch_moba, w_out, ln_mix_g, ln_mix_b,
           w_ffn_gate, w_ffn_up, w_ffn_down, ln_ffn_g, ln_ffn_b):
    batch, seq, d = x.shape
    depth = w_in.shape[0]
    assert w_in.shape[2] == QKV_COLS + 2 * d
    assert seq % MOBA_BLOCK == 0 and seq % SB_TILE == 0
    alpha = (2 * depth) ** 0.25
    tm = min(512, seq)
    cos_t, sina_t, sinb_t = _rope_tables(seq)
    x2 = x.reshape(batch * seq, d)
    for l in range(depth):
        w_l = _bf16(w_in[l])
        qkv = _qkv_proj(x2, w_l[:, :QKV_COLS], cos_t, sina_t, sinb_t, seq, tm)
        o_sb = _sb_attention(qkv, batch, seq)
        o_mb = _moba_attention(qkv, batch, seq)
        x2 = _mix_block(x2, o_sb, o_mb, w_l[:, QKV_COLS:], _bf16(w_branch_sb[l]), _bf16(w_branch_moba[l]),
                        _bf16(w_out[l]), ln_mix_g[l][None, :], ln_mix_b[l][None, :], alpha, tm)
        x2 = _ffn_block(x2, _bf16(w_ffn_gate[l]), _bf16(w_ffn_up[l]), _bf16(w_ffn_down[l]),
                        ln_ffn_g[l][None, :], ln_ffn_b[l][None, :], alpha, tm)
    return x2.reshape(batch, seq, d)
```

```python
import functools

import jax
import jax.numpy as jnp
from jax import lax
from jax.experimental import pallas as pl
from jax.experimental.pallas import tpu as pltpu

HEAD_DIM = 64
SB_HEADS = 8
MOBA_HEADS = 8
SB_WIDTH = SB_HEADS * HEAD_DIM
MOBA_WIDTH = MOBA_HEADS * HEAD_DIM
QKV_COLS = 3 * SB_WIDTH + 3 * MOBA_WIDTH
MOBA_BLOCK = 256
MOBA_TOPK = 3
ROPE_THETA = 500000.0
ROPE_DIM = HEAD_DIM // 4
LN_EPS = 1e-5
SCALE = HEAD_DIM ** -0.5

LANES = 128
HEADS_PER_TILE = LANES // HEAD_DIM
VMEM_LIMIT = 56 * 1024 * 1024
PROJ_TN = 512
SB_TQ = 128
SB_WINDOW = 384
SB_STEP = 128
MOBA_GROUP = 4
NEG_BIG = -1e30
SB_UNDERFLOW = -104.0

_NT = (((1,), (1,)), ((), ()))


def _bf16(a):
    return a.astype(jnp.bfloat16)


def _dot(a, b):
    return jnp.dot(a, b, preferred_element_type=jnp.float32)


def _dot_nt(a, b):
    return lax.dot_general(a, b, _NT, preferred_element_type=jnp.float32)


def _split_dot(a_f32, b_bf16):
    hi = _bf16(a_f32)
    lo = _bf16(a_f32 - hi.astype(jnp.float32))
    return _dot(hi, b_bf16) + _dot(lo, b_bf16)


def _layer_norm(y, g, b):
    mu = jnp.mean(y, axis=-1, keepdims=True)
    d = y - mu
    var = jnp.mean(d * d, axis=-1, keepdims=True)
    return d * lax.rsqrt(var + LN_EPS) * g + b


def _params(n_grid):
    return pltpu.CompilerParams(dimension_semantics=("arbitrary",) * n_grid,
                                vmem_limit_bytes=VMEM_LIMIT)


def _qkv_kernel(x_ref, w_ref, cos_ref, sina_ref, sinb_ref, o_ref, *, rope_lo, rope_hi):
    xb = _bf16(x_ref[...])
    reps = PROJ_TN // LANES
    wide = lambda r: jnp.concatenate([r[...]] * reps, axis=1)
    for j in range(w_ref.shape[1] // PROJ_TN):
        cols = slice(j * PROJ_TN, (j + 1) * PROJ_TN)
        acc = _dot(xb, w_ref[:, cols])
        if rope_lo <= j < rope_hi:
            acc = (acc * wide(cos_ref)
                   + pltpu.roll(acc, ROPE_DIM // 2, 1) * wide(sina_ref)
                   + pltpu.roll(acc, PROJ_TN - ROPE_DIM // 2, 1) * wide(sinb_ref))
        o_ref[:, cols] = _bf16(acc)


def _qkv_proj(x2, w_qkv, cos_t, sina_t, sinb_t, seq, tm):
    t, d = x2.shape
    n = w_qkv.shape[1]
    s_tiles = seq // tm
    tab = pl.BlockSpec((tm, LANES), lambda i: (i % s_tiles, 0))
    rope_lo = 3 * SB_WIDTH // PROJ_TN
    rope_hi = (3 * SB_WIDTH + 2 * MOBA_WIDTH) // PROJ_TN
    return pl.pallas_call(
        functools.partial(_qkv_kernel, rope_lo=rope_lo, rope_hi=rope_hi),
        grid=(t // tm,),
        in_specs=[pl.BlockSpec((tm, d), lambda i: (i, 0)),
                  pl.BlockSpec((d, n), lambda i: (0, 0), pipeline_mode=pl.Buffered(1)),
                  tab, tab, tab],
        out_specs=pl.BlockSpec((tm, n), lambda i: (i, 0)),
        out_shape=jax.ShapeDtypeStruct((t, n), jnp.bfloat16),
        compiler_params=_params(1),
        name="qkv_proj",
    )(x2, w_qkv, cos_t, sina_t, sinb_t)


def _sb_kernel(q_ref, k_ref, v_ref, tri_ref, o_ref):
    i = pl.program_id(2)
    tq, win, step = SB_TQ, SB_WINDOW, SB_STEP
    q = q_ref[...]
    lane = lax.broadcasted_iota(jnp.int32, (tq, LANES), 1)
    qs = [jnp.where(lane // HEAD_DIM == h, q, jnp.zeros_like(q)) * SCALE for h in range(HEADS_PER_TILE)]

    def visit(start, width, state, past):
        k = k_ref[pl.ds(start, width), :]
        v = v_ref[pl.ds(start, width), :]
        tri = tri_ref[:width, :width]
        heads = range(HEADS_PER_TILE)
        zs = [_dot_nt(qs[h], k) for h in heads]
        log_keep, log_beta = [], []
        for h in heads:
            softplus = jnp.maximum(zs[h], 0.0) + jnp.log1p(jnp.exp(-jnp.abs(zs[h])))
            lk = -softplus
            log_keep.append(lk if past is None else jnp.where(past, lk, 0.0))
            log_beta.append(zs[h] - softplus)
        suffix = [_split_dot(log_keep[h], tri) for h in heads]
        ws = []
        for h in heads:
            w = jnp.exp(log_beta[h] + suffix[h] + state[h][0])
            ws.append(_bf16(w if past is None else jnp.where(past, w, 0.0)))
        return tuple((state[h][0] + suffix[h][:, :1] + log_keep[h][:, :1], state[h][1] + _dot(ws[h], v))
                     for h in heads)

    def bound(state):
        return jnp.max(jnp.maximum(state[0][0], state[1][0]))

    first_q = i * tq
    start0 = pl.multiple_of(jnp.maximum(first_q + tq - win, 0), step)
    row = lax.broadcasted_iota(jnp.int32, (tq, win), 0)
    col = lax.broadcasted_iota(jnp.int32, (tq, win), 1)
    past = col - row < first_q - start0
    zero = (jnp.zeros((tq, 1), jnp.float32), jnp.zeros((tq, LANES), jnp.float32))
    first = visit(start0, win, (zero,) * HEADS_PER_TILE, past)

    def cond(loop):
        j, worst, _ = loop
        return (j >= 0) & (worst > SB_UNDERFLOW)

    def body(loop):
        j, _, state = loop
        state = visit(pl.multiple_of(j * step, step), step, state, None)
        return j - 1, bound(state), state

    _, _, state = lax.while_loop(cond, body, (start0 // step - 1, bound(first), first))
    o_ref[...] = _bf16(jnp.where(lane < HEAD_DIM, state[0][1], state[1][1]))


def _sb_attention(qkv, batch, seq):
    t = qkv.shape[0]
    assert seq % SB_TQ == 0 and seq >= SB_WINDOW
    nq = seq // SB_TQ
    hp = SB_WIDTH // LANES
    idx = jnp.arange(SB_WINDOW)
    tri = _bf16(idx[:, None] > idx[None, :])
    return pl.pallas_call(
        _sb_kernel,
        grid=(batch, hp, nq),
        in_specs=[pl.BlockSpec((SB_TQ, LANES), lambda b, p, i: (b * nq + i, p)),
                  pl.BlockSpec((seq, LANES), lambda b, p, i: (b, hp + p)),
                  pl.BlockSpec((seq, LANES), lambda b, p, i: (b, 2 * hp + p)),
                  pl.BlockSpec((SB_WINDOW, SB_WINDOW), lambda b, p, i: (0, 0))],
        out_specs=pl.BlockSpec((SB_TQ, LANES), lambda b, p, i: (b * nq + i, p)),
        out_shape=jax.ShapeDtypeStruct((t, SB_WIDTH), jnp.bfloat16),
        compiler_params=_params(3),
        name="sb_attention",
    )(qkv, qkv, qkv, tri)


def _moba_kernel(q_ref, k_ref, v_ref, o_ref, kmean_ref, kaug_ref, vt_ref, *, n_blocks):
    i = pl.program_id(2)
    bs = MOBA_BLOCK
    hd = HEAD_DIM
    grp = MOBA_GROUP
    lane = lax.broadcasted_iota(jnp.int32, (bs, LANES), 1)

    @pl.when(i == 0)
    def _():
        kmean_ref[...] = jnp.zeros_like(kmean_ref)
        for n in range(n_blocks):
            kb = k_ref[n * bs:(n + 1) * bs, :].astype(jnp.float32)
            kmean_ref[hd + n:hd + n + 1, :] = jnp.sum(kb, axis=0, keepdims=True) * (1.0 / bs)
            tag = (lane == hd + n).astype(jnp.float32)
            for h in range(HEADS_PER_TILE):
                dims = kb if h == 0 else pltpu.roll(kb, LANES - h * hd, 1)
                kaug_ref[h, n * bs:(n + 1) * bs, :] = _bf16(jnp.where(lane < hd, dims, tag))
            vt_ref[n] = _bf16(v_ref[n * bs:(n + 1) * bs, :].astype(jnp.float32).T)

    q = q_ref[...]
    qf = q.astype(jnp.float32)
    key = lax.broadcasted_iota(jnp.int32, (bs, bs), 0)
    qry = lax.broadcasted_iota(jnp.int32, (bs, bs), 1)
    kmean = kmean_ref[...]
    km_hi = _bf16(kmean)
    km_lo = _bf16(kmean - km_hi.astype(jnp.float32))
    own_start = pl.multiple_of(i * bs, bs)
    vt_own = vt_ref[i]

    q_aug = []
    state = []
    for h in range(HEADS_PER_TILE):
        qh = jnp.where(lane // hd == h, q, jnp.zeros_like(q))

        gate = _dot_nt(qh, km_hi) + _dot_nt(qh, km_lo)
        gate = jnp.where((lane >= hd) & (lane < hd + i), gate, -jnp.inf)
        sel = jnp.zeros((bs, LANES), jnp.bool_)
        for _ in range(MOBA_TOPK):
            best = jnp.max(gate, axis=1, keepdims=True)
            first = jnp.min(jnp.where(gate == best, lane, LANES), axis=1, keepdims=True)
            pick = (lane == first) & (best > -jnp.inf)
            sel = sel | pick
            gate = jnp.where(pick, -jnp.inf, gate)

        dims = (qf if h == 0 else pltpu.roll(qf, LANES - h * hd, 1)) * SCALE
        q_aug.append(_bf16(jnp.where(lane < hd, dims, jnp.where(sel, 0.0, NEG_BIG))))
        q_own = _bf16(jnp.where(lane < hd, dims, 0.0))

        s = jnp.where(key <= qry, _dot_nt(kaug_ref[h, pl.ds(own_start, bs), :], q_own), NEG_BIG)
        m = jnp.max(s, axis=0, keepdims=True)
        p = jnp.exp(s - m)
        state.append((m, jnp.sum(p, axis=0, keepdims=True), _dot(vt_own[h * hd:(h + 1) * hd, :], _bf16(p))))

    def scores(blk):
        start = pl.multiple_of(jnp.minimum(blk, n_blocks - 1) * bs, bs)
        return tuple(_dot_nt(kaug_ref[h, pl.ds(start, bs), :], q_aug[h]) for h in range(HEADS_PER_TILE))

    def past_group(g, carry):
        state, s_next = carry
        first_blk = g * grp
        state = list(state)
        for c in range(grp):
            s_cur, s_next = s_next, scores(first_blk + c + 1)
            for h in range(HEADS_PER_TILE):
                m_old, l_old, acc_old = state[h]
                s = s_cur[h]
                m_new = jnp.maximum(m_old, jnp.max(s, axis=0, keepdims=True))
                p = jnp.exp(s - m_new)
                alpha = jnp.exp(m_old - m_new)
                state[h] = (m_new, alpha * l_old + jnp.sum(p, axis=0, keepdims=True),
                            alpha * acc_old + _dot(vt_ref[first_blk + c, h * hd:(h + 1) * hd, :], _bf16(p)))
        return tuple(state), s_next

    state, _ = lax.fori_loop(0, (i + grp - 1) // grp, past_group, (tuple(state), scores(0)))
    out_t = jnp.concatenate([acc / l for (_, l, acc) in state], axis=0)
    o_ref[...] = _bf16(out_t.T)


def _moba_attention(qkv, batch, seq):
    t = qkv.shape[0]
    nq = seq // MOBA_BLOCK
    assert nq % MOBA_GROUP == 0, "the grouped block walk reads whole groups"
    assert nq <= LANES - HEAD_DIM, "one block tag lane per key block"
    hp = MOBA_WIDTH // LANES
    base = 3 * SB_WIDTH // LANES
    return pl.pallas_call(
        functools.partial(_moba_kernel, n_blocks=nq),
        grid=(batch, hp, nq),
        in_specs=[pl.BlockSpec((MOBA_BLOCK, LANES), lambda b, p, i: (b * nq + i, base + p)),
                  pl.BlockSpec((seq, LANES), lambda b, p, i: (b, base + hp + p)),
                  pl.BlockSpec((seq, LANES), lambda b, p, i: (b, base + 2 * hp + p))],
        out_specs=pl.BlockSpec((MOBA_BLOCK, LANES), lambda b, p, i: (b * nq + i, p)),
        out_shape=jax.ShapeDtypeStruct((t, MOBA_WIDTH), jnp.bfloat16),
        scratch_shapes=[pltpu.VMEM((LANES, LANES), jnp.float32),
                        pltpu.VMEM((HEADS_PER_TILE, seq, LANES), jnp.bfloat16),
                        pltpu.VMEM((nq, LANES, MOBA_BLOCK), jnp.bfloat16)],
        compiler_params=_params(3),
        name="moba_attention",
    )(qkv, qkv, qkv)


def _mix_kernel(x_ref, osb_ref, omb_ref, wg_ref, wbs_ref, wbm_ref, wo_ref, g_ref, b_ref, o_ref, *, alpha):
    x = x_ref[...]
    d = x.shape[1]
    gates = _dot(_bf16(x), wg_ref[...])
    branch_sb = _dot(osb_ref[...], wbs_ref[...])
    branch_mb = _dot(omb_ref[...], wbm_ref[...])
    merged = jax.nn.sigmoid(gates[:, :d]) * branch_sb + jax.nn.sigmoid(gates[:, d:]) * branch_mb
    mix = _dot(_bf16(merged), wo_ref[...])
    o_ref[...] = _layer_norm(alpha * x + mix, g_ref[...], b_ref[...])


def _mix_block(x2, o_sb, o_mb, w_gate, w_bsb, w_bmb, w_out, ln_g, ln_b, alpha, tm):
    t, d = x2.shape
    rows = lambda w: pl.BlockSpec((tm, w), lambda i: (i, 0))
    whole = lambda a: pl.BlockSpec(a.shape, lambda i: (0, 0), pipeline_mode=pl.Buffered(1))
    return pl.pallas_call(
        functools.partial(_mix_kernel, alpha=alpha),
        grid=(t // tm,),
        in_specs=[rows(d), rows(SB_WIDTH), rows(MOBA_WIDTH),
                  whole(w_gate), whole(w_bsb), whole(w_bmb), whole(w_out), whole(ln_g), whole(ln_b)],
        out_specs=rows(d),
        out_shape=jax.ShapeDtypeStruct((t, d), jnp.float32),
        compiler_params=_params(1),
        name="mix_ln",
    )(x2, o_sb, o_mb, w_gate, w_bsb, w_bmb, w_out, ln_g, ln_b)


def _ffn_kernel(x_ref, wg_ref, wu_ref, wd_ref, g_ref, b_ref, o_ref, *, alpha, chunk):
    x = x_ref[...]
    xb = _bf16(x)
    acc = alpha * x
    for c in range(0, wg_ref.shape[1], chunk):
        gate = _dot(xb, wg_ref[:, c:c + chunk])
        up = _dot(xb, wu_ref[:, c:c + chunk])
        acc = acc + _dot(_bf16(jax.nn.silu(gate) * up), wd_ref[c:c + chunk, :])
    o_ref[...] = _layer_norm(acc, g_ref[...], b_ref[...])


def _ffn_chunk(d_ff):
    best = LANES
    for c in range(LANES, 1536 + 1, LANES):
        if d_ff % c == 0:
            best = c
    return best


def _ffn_block(x2, w_gate, w_up, w_down, ln_g, ln_b, alpha, tm):
    t, d = x2.shape
    rows = pl.BlockSpec((tm, d), lambda i: (i, 0))
    whole = lambda a: pl.BlockSpec(a.shape, lambda i: (0, 0), pipeline_mode=pl.Buffered(1))
    return pl.pallas_call(
        functools.partial(_ffn_kernel, alpha=alpha, chunk=_ffn_chunk(w_gate.shape[1])),
        grid=(t // tm,),
        in_specs=[rows, whole(w_gate), whole(w_up), whole(w_down), whole(ln_g), whole(ln_b)],
        out_specs=rows,
        out_shape=jax.ShapeDtypeStruct((t, d), jnp.float32),
        compiler_params=_params(1),
        name="ffn_ln",
    )(x2, w_gate, w_up, w_down, ln_g, ln_b)


def _rope_tables(seq):
    half = ROPE_DIM // 2
    inv_freq = ROPE_THETA ** (-jnp.arange(0, ROPE_DIM, 2, dtype=jnp.float32) / ROPE_DIM)
    ang = jnp.arange(seq).astype(jnp.float32)[:, None] * inv_freq[None, :]
    cos, sin = jnp.cos(ang), jnp.sin(ang)
    rest = HEAD_DIM - ROPE_DIM
    ones = jnp.ones((seq, rest), jnp.float32)
    zeros = jnp.zeros((seq, rest), jnp.float32)
    zh = jnp.zeros((seq, half), jnp.float32)
    per_head = lambda parts: jnp.concatenate(parts * HEADS_PER_TILE, axis=1)
    return (per_head([cos, cos, ones]), per_head([zh, sin, zeros]), per_head([-sin, zh, zeros]))


def kernel(x, w_in, w_branch_sb, w_branch_moba, w_out, ln_mix_g, ln_mix_b,
           w_ffn_gate, w_ffn_up, w_ffn_down, ln_ffn_g, ln_ffn_b):
    batch, seq, d = x.shape
    depth = w_in.shape[0]
    assert w_in.shape[2] == QKV_COLS + 2 * d
    assert seq % MOBA_BLOCK == 0
    alpha = (2 * depth) ** 0.25
    tm = min(512, seq)
    cos_t, sina_t, sinb_t = _rope_tables(seq)
    x2 = x.reshape(batch * seq, d)
    for l in range(depth):
        w_l = _bf16(w_in[l])
        qkv = _qkv_proj(x2, w_l[:, :QKV_COLS], cos_t, sina_t, sinb_t, seq, tm)
        o_sb = _sb_attention(qkv, batch, seq)
        o_mb = _moba_attention(qkv, batch, seq)
        x2 = _mix_block(x2, o_sb, o_mb, w_l[:, QKV_COLS:], _bf16(w_branch_sb[l]), _bf16(w_branch_moba[l]),
                        _bf16(w_out[l]), ln_mix_g[l][None, :], ln_mix_b[l][None, :], alpha, tm)
        x2 = _ffn_block(x2, _bf16(w_ffn_gate[l]), _bf16(w_ffn_up[l]), _bf16(w_ffn_down[l]),
                        ln_ffn_g[l][None, :], ln_ffn_b[l][None, :], alpha, tm)
    return x2.reshape(batch, seq, d)
```

```python
import functools

import jax
import jax.numpy as jnp
from jax import lax
from jax.experimental import pallas as pl
from jax.experimental.pallas import tpu as pltpu

HEAD_DIM = 64
SB_HEADS = 8
MOBA_HEADS = 8
SB_WIDTH = SB_HEADS * HEAD_DIM
MOBA_WIDTH = MOBA_HEADS * HEAD_DIM
QKV_COLS = 3 * SB_WIDTH + 3 * MOBA_WIDTH
MOBA_BLOCK = 256
MOBA_TOPK = 3
ROPE_THETA = 500000.0
ROPE_DIM = HEAD_DIM // 4
LN_EPS = 1e-5
SCALE = HEAD_DIM ** -0.5

LANES = 128
HEADS_PER_TILE = LANES // HEAD_DIM
VMEM_LIMIT = 56 * 1024 * 1024
PROJ_TN = 512
SB_TQ = 128
SB_TILES = 2
SB_WINDOW = 384
SB_STEP = 128
MOBA_GROUP = 4
MOBA_TAGS = 32
MOBA_SHIFT_MAX = 60.0
MOBA_NORM_SLACK = 1.01
NEG_BIG = -1e30
SB_UNDERFLOW = -104.0

_NT = (((1,), (1,)), ((), ()))


def _bf16(a):
    return a.astype(jnp.bfloat16)


def _dot(a, b):
    return jnp.dot(a, b, preferred_element_type=jnp.float32)


def _dot_nt(a, b):
    return lax.dot_general(a, b, _NT, preferred_element_type=jnp.float32)


def _split_dot(a_f32, tri2_ref, width):
    hi = _bf16(a_f32)
    lo = _bf16(a_f32 - hi.astype(jnp.float32))
    if width == SB_WINDOW:
        return _dot(jnp.concatenate([hi, lo], axis=1), tri2_ref[...])
    tri = tri2_ref[:width, :width]
    return _dot(jnp.concatenate([hi, lo], axis=1), jnp.concatenate([tri, tri], axis=0))


def _layer_norm(y, g, b):
    mu = jnp.mean(y, axis=-1, keepdims=True)
    d = y - mu
    var = jnp.mean(d * d, axis=-1, keepdims=True)
    return d * lax.rsqrt(var + LN_EPS) * g + b


def _params(n_grid):
    return pltpu.CompilerParams(dimension_semantics=("arbitrary",) * n_grid,
                                vmem_limit_bytes=VMEM_LIMIT)


def _qkv_kernel(x_ref, w_ref, cos_ref, sina_ref, sinb_ref, o_ref, *, rope_lo, rope_hi):
    xb = _bf16(x_ref[...])
    reps = PROJ_TN // LANES
    wide = lambda r: jnp.concatenate([r[...]] * reps, axis=1)
    for j in range(w_ref.shape[1] // PROJ_TN):
        cols = slice(j * PROJ_TN, (j + 1) * PROJ_TN)
        acc = _dot(xb, w_ref[:, cols])
        if rope_lo <= j < rope_hi:
            acc = (acc * wide(cos_ref)
                   + pltpu.roll(acc, ROPE_DIM // 2, 1) * wide(sina_ref)
                   + pltpu.roll(acc, PROJ_TN - ROPE_DIM // 2, 1) * wide(sinb_ref))
        o_ref[:, cols] = _bf16(acc)


def _qkv_proj(x2, w_qkv, cos_t, sina_t, sinb_t, seq, tm):
    t, d = x2.shape
    n = w_qkv.shape[1]
    s_tiles = seq // tm
    tab = pl.BlockSpec((tm, LANES), lambda i: (i % s_tiles, 0))
    rope_lo = 3 * SB_WIDTH // PROJ_TN
    rope_hi = (3 * SB_WIDTH + 2 * MOBA_WIDTH) // PROJ_TN
    return pl.pallas_call(
        functools.partial(_qkv_kernel, rope_lo=rope_lo, rope_hi=rope_hi),
        grid=(t // tm,),
        in_specs=[pl.BlockSpec((tm, d), lambda i: (i, 0)),
                  pl.BlockSpec((d, n), lambda i: (0, 0), pipeline_mode=pl.Buffered(1)),
                  tab, tab, tab],
        out_specs=pl.BlockSpec((tm, n), lambda i: (i, 0)),
        out_shape=jax.ShapeDtypeStruct((t, n), jnp.bfloat16),
        compiler_params=_params(1),
        name="qkv_proj",
    )(x2, w_qkv, cos_t, sina_t, sinb_t)


def _sb_kernel(q_ref, k_ref, v_ref, tri_ref, o_ref):
    i = pl.program_id(2)
    tq, win, step = SB_TQ, SB_WINDOW, SB_STEP
    lane = lax.broadcasted_iota(jnp.int32, (tq, LANES), 1)
    row = lax.broadcasted_iota(jnp.int32, (tq, win), 0)
    col = lax.broadcasted_iota(jnp.int32, (tq, win), 1)

    def visit(jobs):
        n = range(len(jobs))
        ks = [k_ref[pl.ds(start, width), :] for (_, start, width, _, _) in jobs]
        zs = [_dot_nt(jobs[u][0], ks[u]) for u in n]
        log_keep, log_beta = [], []
        for u in n:
            past = jobs[u][4]
            lb = jnp.minimum(zs[u], 0.0) - jnp.log(1.0 + jnp.exp(-jnp.abs(zs[u])))
            lk = lb - zs[u]
            log_keep.append(lk if past is None else jnp.where(past, lk, 0.0))
            log_beta.append(lb)
        suffix = [_split_dot(log_keep[u], tri_ref, jobs[u][2]) for u in n]
        out = []
        for u in n:
            _, start, width, (carry, acc), past = jobs[u]
            w = jnp.exp(log_beta[u] + suffix[u] + carry)
            w = _bf16(w if past is None else jnp.where(past, w, 0.0))
            out.append((carry + suffix[u][:, :1] + log_keep[u][:, :1],
                        acc + _dot(w, v_ref[pl.ds(start, width), :])))
        return out

    zero = (jnp.zeros((tq, 1), jnp.float32), jnp.zeros((tq, LANES), jnp.float32))
    jobs, starts, queries = [], [], []
    for t in range(SB_TILES):
        q = q_ref[t * tq:(t + 1) * tq, :]
        first_q = (i * SB_TILES + t) * tq
        start0 = pl.multiple_of(jnp.maximum(first_q + tq - win, 0), step)
        past = col - row < first_q - start0
        qs = [jnp.where(lane // HEAD_DIM == h, q, jnp.zeros_like(q)) * SCALE for h in range(HEADS_PER_TILE)]
        jobs += [(qs[h], start0, win, zero, past) for h in range(HEADS_PER_TILE)]
        starts.append(start0)
        queries.append(qs)
    first = visit(jobs)

    for t in range(SB_TILES):
        state = tuple(first[t * HEADS_PER_TILE:(t + 1) * HEADS_PER_TILE])
        qs = queries[t]

        def bound(state):
            return jnp.max(jnp.maximum(state[0][0], state[1][0]))

        def cond(loop):
            j, worst, _ = loop
            return (j >= 0) & (worst > SB_UNDERFLOW)

        def body(loop, qs=qs):
            j, _, state = loop
            start = pl.multiple_of(j * step, step)
            state = tuple(visit([(qs[h], start, step, state[h], None) for h in range(HEADS_PER_TILE)]))
            return j - 1, bound(state), state

        _, _, state = lax.while_loop(cond, body, (starts[t] // step - 1, bound(state), state))
        o_ref[t * tq:(t + 1) * tq, :] = _bf16(jnp.where(lane < HEAD_DIM, state[0][1], state[1][1]))


def _sb_attention(qkv, batch, seq):
    t = qkv.shape[0]
    rows = SB_TQ * SB_TILES
    assert seq % rows == 0 and seq >= SB_WINDOW
    nq = seq // rows
    hp = SB_WIDTH // LANES
    idx = jnp.arange(SB_WINDOW)
    tri = _bf16(idx[:, None] > idx[None, :])
    tri2 = jnp.concatenate([tri, tri], axis=0)
    return pl.pallas_call(
        _sb_kernel,
        grid=(batch, hp, nq),
        in_specs=[pl.BlockSpec((rows, LANES), lambda b, p, i: (b * nq + i, p)),
                  pl.BlockSpec((seq, LANES), lambda b, p, i: (b, hp + p)),
                  pl.BlockSpec((seq, LANES), lambda b, p, i: (b, 2 * hp + p)),
                  pl.BlockSpec((2 * SB_WINDOW, SB_WINDOW), lambda b, p, i: (0, 0))],
        out_specs=pl.BlockSpec((rows, LANES), lambda b, p, i: (b * nq + i, p)),
        out_shape=jax.ShapeDtypeStruct((t, SB_WIDTH), jnp.bfloat16),
        compiler_params=_params(3),
        name="sb_attention",
    )(qkv, qkv, qkv, tri2)


def _moba_kernel(q_ref, k_ref, v_ref, o_ref, kmean_ref, knorm_ref, kaug_ref, vt_ref, *, n_blocks):
    i = pl.program_id(2)
    bs = MOBA_BLOCK
    hd = HEAD_DIM
    grp = MOBA_GROUP
    heads = range(HEADS_PER_TILE)
    lane = lax.broadcasted_iota(jnp.int32, (bs, LANES), 1)

    @pl.when(i == 0)
    def _():
        kmean_ref[...] = jnp.zeros_like(kmean_ref)
        longest = [jnp.zeros((1, 1), jnp.float32)] * HEADS_PER_TILE
        ones = jnp.ones((8, LANES), jnp.bfloat16)
        for n in range(n_blocks):
            kb = k_ref[n * bs:(n + 1) * bs, :].astype(jnp.float32)
            kmean_ref[hd + n:hd + n + 1, :] = jnp.sum(kb, axis=0, keepdims=True) * (1.0 / bs)
            tag = (lane == hd + n).astype(jnp.float32)
            for h in heads:
                dims = kb if h == 0 else pltpu.roll(kb, LANES - h * hd, 1)
                kaug_ref[h, n * bs:(n + 1) * bs, :] = _bf16(jnp.where(lane < hd, dims, tag))
                norm2 = _dot_nt(ones, _bf16(jnp.where(lane < hd, dims * dims, 0.0)))[:1]
                longest[h] = jnp.maximum(longest[h], jnp.sqrt(jnp.max(norm2, axis=1, keepdims=True)))
                knorm_ref[h, n:n + 1, :] = jnp.broadcast_to(longest[h], (1, LANES))
            vt_ref[n] = _bf16(v_ref[n * bs:(n + 1) * bs, :].astype(jnp.float32).T)

    q = q_ref[...]
    qf = q.astype(jnp.float32)
    key = lax.broadcasted_iota(jnp.int32, (bs, bs), 0)
    qry = lax.broadcasted_iota(jnp.int32, (bs, bs), 1)
    rows = lax.broadcasted_iota(jnp.int32, (MOBA_TAGS, bs), 0)
    causal = key <= qry
    kmean = kmean_ref[...]
    km_hi = _bf16(kmean[hd:hd + MOBA_TAGS])
    km_lo = _bf16(kmean[hd:hd + MOBA_TAGS] - km_hi.astype(jnp.float32))
    own_start = pl.multiple_of(i * bs, bs)
    vt_own = vt_ref[i]

    dims, sel, shift = [], [], []
    for h in heads:
        qh = jnp.where(lane // hd == h, q, jnp.zeros_like(q))
        gate = _dot_nt(km_hi, qh) + _dot_nt(km_lo, qh)
        gate = jnp.where(rows < i, gate, -jnp.inf)
        chosen = jnp.zeros((MOBA_TAGS, bs), jnp.float32)
        for _ in range(MOBA_TOPK):
            best = jnp.max(gate, axis=0, keepdims=True)
            first = jnp.min(jnp.where(gate == best, rows, MOBA_TAGS), axis=0, keepdims=True)
            pick = (rows == first) & (best > -jnp.inf)
            chosen = jnp.where(pick, 1.0, chosen)
            gate = jnp.where(pick, -jnp.inf, gate)
        pad = jnp.zeros((hd, bs), jnp.float32)
        by_lane = jnp.concatenate([pad, chosen, jnp.zeros((LANES - hd - MOBA_TAGS, bs), jnp.float32)], axis=0).T
        sel.append(by_lane > 0.0)
        d = (qf if h == 0 else pltpu.roll(qf, LANES - h * hd, 1)) * SCALE
        dims.append(d)
        norm = jnp.sqrt(jnp.sum(jnp.where(lane < hd, d * d, 0.0), axis=1, keepdims=True))
        shift.append(norm * knorm_ref[h, pl.ds(i, 1), :] * MOBA_NORM_SLACK)

    def finish(state):
        out_t = jnp.concatenate([acc / l for (l, acc) in state], axis=0)
        return out_t.T

    def walk(q_past, step, state):
        def scores(blk):
            start = pl.multiple_of(jnp.minimum(blk, n_blocks - 1) * bs, bs)
            return tuple(_dot_nt(kaug_ref[h, pl.ds(start, bs), :], q_past[h]) for h in heads)

        def past_group(g, carry):
            state, s_next = carry
            state = list(state)
            for c in range(grp):
                s_cur, s_next = s_next, scores(g * grp + c + 1)
                for h in heads:
                    state[h] = step(state[h], s_cur[h], vt_ref[g * grp + c, h * hd:(h + 1) * hd, :])
            return tuple(state), s_next

        state, _ = lax.fori_loop(0, (i + grp - 1) // grp, past_group, (tuple(state), scores(0)))
        return state

    def shifted():
        q_past, state = [], []
        for h in heads:
            q_past.append(_bf16(jnp.where(lane < hd, dims[h], jnp.where(sel[h], -shift[h], NEG_BIG))))
            q_own = _bf16(jnp.where(lane < hd, dims[h], jnp.where(lane == hd + i, -shift[h], 0.0)))
            s = _dot_nt(kaug_ref[h, pl.ds(own_start, bs), :], q_own)
            p = jnp.where(causal, jnp.exp(s), 0.0)
            state.append((jnp.sum(p, axis=0, keepdims=True), _dot(vt_own[h * hd:(h + 1) * hd, :], _bf16(p))))

        def step(st, s, vt):
            p = jnp.exp(s)
            return st[0] + jnp.sum(p, axis=0, keepdims=True), st[1] + _dot(vt, _bf16(p))

        return finish(walk(q_past, step, state))

    def running_max():
        q_past, state = [], []
        for h in heads:
            q_past.append(_bf16(jnp.where(lane < hd, dims[h], jnp.where(sel[h], 0.0, NEG_BIG))))
            q_own = _bf16(jnp.where(lane < hd, dims[h], 0.0))
            s = jnp.where(causal, _dot_nt(kaug_ref[h, pl.ds(own_start, bs), :], q_own), NEG_BIG)
            m = jnp.max(s, axis=0, keepdims=True)
            p = jnp.exp(s - m)
            state.append((m, jnp.sum(p, axis=0, keepdims=True), _dot(vt_own[h * hd:(h + 1) * hd, :], _bf16(p))))

        def step(st, s, vt):
            m_old, l_old, acc_old = st
            m_new = jnp.maximum(m_old, jnp.max(s, axis=0, keepdims=True))
            p = jnp.exp(s - m_new)
            alpha = jnp.exp(m_old - m_new)
            return m_new, alpha * l_old + jnp.sum(p, axis=0, keepdims=True), alpha * acc_old + _dot(vt, _bf16(p))

        return finish([st[1:] for st in walk(q_past, step, state)])

    largest = jnp.max(jnp.maximum(shift[0], shift[1]))
    o_ref[...] = _bf16(lax.cond(2.0 * largest <= MOBA_SHIFT_MAX, shifted, running_max))


def _moba_attention(qkv, batch, seq):
    t = qkv.shape[0]
    nq = seq // MOBA_BLOCK
    assert nq % MOBA_GROUP == 0, "the grouped block walk reads whole groups"
    assert nq <= MOBA_TAGS, "one block tag lane per key block"
    hp = MOBA_WIDTH // LANES
    base = 3 * SB_WIDTH // LANES
    return pl.pallas_call(
        functools.partial(_moba_kernel, n_blocks=nq),
        grid=(batch, hp, nq),
        in_specs=[pl.BlockSpec((MOBA_BLOCK, LANES), lambda b, p, i: (b * nq + i, base + p)),
                  pl.BlockSpec((seq, LANES), lambda b, p, i: (b, base + hp + p)),
                  pl.BlockSpec((seq, LANES), lambda b, p, i: (b, base + 2 * hp + p))],
        out_specs=pl.BlockSpec((MOBA_BLOCK, LANES), lambda b, p, i: (b * nq + i, p)),
        out_shape=jax.ShapeDtypeStruct((t, MOBA_WIDTH), jnp.bfloat16),
        scratch_shapes=[pltpu.VMEM((LANES, LANES), jnp.float32),
                        pltpu.VMEM((HEADS_PER_TILE, nq, LANES), jnp.float32),
                        pltpu.VMEM((HEADS_PER_TILE, seq, LANES), jnp.bfloat16),
                        pltpu.VMEM((nq, LANES, MOBA_BLOCK), jnp.bfloat16)],
        compiler_params=_params(3),
        name="moba_attention",
    )(qkv, qkv, qkv)


def _mix_kernel(x_ref, osb_ref, omb_ref, wg_ref, wbs_ref, wbm_ref, wo_ref, g_ref, b_ref, o_ref, *, alpha):
    x = x_ref[...]
    d = x.shape[1]
    gates = _dot(_bf16(x), wg_ref[...])
    branch_sb = _dot(osb_ref[...], wbs_ref[...])
    branch_mb = _dot(omb_ref[...], wbm_ref[...])
    merged = jax.nn.sigmoid(gates[:, :d]) * branch_sb + jax.nn.sigmoid(gates[:, d:]) * branch_mb
    mix = _dot(_bf16(merged), wo_ref[...])
    o_ref[...] = _layer_norm(alpha * x + mix, g_ref[...], b_ref[...])


def _mix_block(x2, o_sb, o_mb, w_gate, w_bsb, w_bmb, w_out, ln_g, ln_b, alpha, tm):
    t, d = x2.shape
    rows = lambda w: pl.BlockSpec((tm, w), lambda i: (i, 0))
    whole = lambda a: pl.BlockSpec(a.shape, lambda i: (0, 0), pipeline_mode=pl.Buffered(1))
    return pl.pallas_call(
        functools.partial(_mix_kernel, alpha=alpha),
        grid=(t // tm,),
        in_specs=[rows(d), rows(SB_WIDTH), rows(MOBA_WIDTH),
                  whole(w_gate), whole(w_bsb), whole(w_bmb), whole(w_out), whole(ln_g), whole(ln_b)],
        out_specs=rows(d),
        out_shape=jax.ShapeDtypeStruct((t, d), jnp.float32),
        compiler_params=_params(1),
        name="mix_ln",
    )(x2, o_sb, o_mb, w_gate, w_bsb, w_bmb, w_out, ln_g, ln_b)


def _ffn_kernel(x_ref, wg_ref, wu_ref, wd_ref, g_ref, b_ref, o_ref, *, alpha, chunk):
    x = x_ref[...]
    xb = _bf16(x)
    acc = alpha * x
    for c in range(0, wg_ref.shape[1], chunk):
        gate = _dot(xb, wg_ref[:, c:c + chunk])
        up = _dot(xb, wu_ref[:, c:c + chunk])
        acc = acc + _dot(_bf16(jax.nn.silu(gate) * up), wd_ref[c:c + chunk, :])
    o_ref[...] = _layer_norm(acc, g_ref[...], b_ref[...])


def _ffn_chunk(d_ff):
    best = LANES
    for c in range(LANES, 1536 + 1, LANES):
        if d_ff % c == 0:
            best = c
    return best


def _ffn_block(x2, w_gate, w_up, w_down, ln_g, ln_b, alpha, tm):
    t, d = x2.shape
    rows = pl.BlockSpec((tm, d), lambda i: (i, 0))
    whole = lambda a: pl.BlockSpec(a.shape, lambda i: (0, 0), pipeline_mode=pl.Buffered(1))
    return pl.pallas_call(
        functools.partial(_ffn_kernel, alpha=alpha, chunk=_ffn_chunk(w_gate.shape[1])),
        grid=(t // tm,),
        in_specs=[rows, whole(w_gate), whole(w_up), whole(w_down), whole(ln_g), whole(ln_b)],
        out_specs=rows,
        out_shape=jax.ShapeDtypeStruct((t, d), jnp.float32),
        compiler_params=_params(1),
        name="ffn_ln",
    )(x2, w_gate, w_up, w_down, ln_g, ln_b)


def _rope_tables(seq):
    half = ROPE_DIM // 2
    inv_freq = ROPE_THETA ** (-jnp.arange(0, ROPE_DIM, 2, dtype=jnp.float32) / ROPE_DIM)
    ang = jnp.arange(seq).astype(jnp.float32)[:, None] * inv_freq[None, :]
    cos, sin = jnp.cos(ang), jnp.sin(ang)
    rest = HEAD_DIM - ROPE_DIM
    ones = jnp.ones((seq, rest), jnp.float32)
    zeros = jnp.zeros((seq, rest), jnp.float32)
    zh = jnp.zeros((seq, half), jnp.float32)
    per_head = lambda parts: jnp.concatenate(parts * HEADS_PER_TILE, axis=1)
    return (per_head([cos, cos, ones]), per_head([zh, sin, zeros]), per_head([-sin, zh, zeros]))


def kernel(x, w_in, w_branch_sb, w_branch_moba, w_out, ln_mix_g, ln_mix_b,
           w_ffn_gate, w_ffn_up, w_ffn_down, ln_ffn_g, ln_ffn_b):
    batch, seq, d = x.shape
    depth = w_in.shape[0]
    assert w_in.shape[2] == QKV_COLS + 2 * d
    assert seq % MOBA_BLOCK == 0
    alpha = (2 * depth) ** 0.25
    tm = min(512, seq)
    cos_t, sina_t, sinb_t = _rope_tables(seq)
    x2 = x.reshape(batch * seq, d)
    for l in range(depth):
        w_l = _bf16(w_in[l])
        qkv = _qkv_proj(x2, w_l[:, :QKV_COLS], cos_t, sina_t, sinb_t, seq, tm)
        o_sb = _sb_attention(qkv, batch, seq)
        o_mb = _moba_attention(qkv, batch, seq)
        x2 = _mix_block(x2, o_sb, o_mb, w_l[:, QKV_COLS:], _bf16(w_branch_sb[l]), _bf16(w_branch_moba[l]),
                        _bf16(w_out[l]), ln_mix_g[l][None, :], ln_mix_b[l][None, :], alpha, tm)
        x2 = _ffn_block(x2, _bf16(w_ffn_gate[l]), _bf16(w_ffn_up[l]), _bf16(w_ffn_down[l]),
                        ln_ffn_g[l][None, :], ln_ffn_b[l][None, :], alpha, tm)
    return x2.reshape(batch, seq, d)
```

```python
import functools

import jax
import jax.numpy as jnp
import numpy as np
from jax import lax
from jax.experimental import pallas as pl
from jax.experimental.pallas import tpu as pltpu

HEAD_DIM = 64
SB_HEADS = 8
MOBA_HEADS = 8
SB_WIDTH = SB_HEADS * HEAD_DIM
MOBA_WIDTH = MOBA_HEADS * HEAD_DIM
QKV_COLS = 3 * SB_WIDTH + 3 * MOBA_WIDTH
MOBA_BLOCK = 256
MOBA_TOPK = 3
ROPE_THETA = 500000.0
ROPE_DIM = HEAD_DIM // 4
LN_EPS = 1e-5
SCALE = HEAD_DIM ** -0.5

LANES = 128
HEADS_PER_TILE = LANES // HEAD_DIM
VMEM_LIMIT = 56 * 1024 * 1024
PROJ_TN = 512
SB_TQ = 64
SB_TILES = 4
SB_WINDOW = 256
SB_STEP = 64
MOBA_GROUP = 4
MOBA_PAIRS = 2
MOBA_TAGS = 32
MOBA_SHIFT_MAX = 60.0
MOBA_NORM_SLACK = 1.01
NEG_BIG = -1e30
SB_UNDERFLOW = -104.0

_NT = (((1,), (1,)), ((), ()))


def _bf16(a):
    return a.astype(jnp.bfloat16)


def _dot(a, b):
    return jnp.dot(a, b, preferred_element_type=jnp.float32)


def _dot_nt(a, b):
    return lax.dot_general(a, b, _NT, preferred_element_type=jnp.float32)


def _split_dot(a_f32, tri2_ref, width):
    hi = _bf16(a_f32)
    lo = _bf16(a_f32 - hi.astype(jnp.float32))
    if width == SB_WINDOW:
        return _dot(jnp.concatenate([hi, lo], axis=1), tri2_ref[...])
    tri = tri2_ref[:width, :width]
    return _dot(jnp.concatenate([hi, lo], axis=1), jnp.concatenate([tri, tri], axis=0))


def _layer_norm(y, g, b):
    mu = jnp.mean(y, axis=-1, keepdims=True)
    d = y - mu
    var = jnp.mean(d * d, axis=-1, keepdims=True)
    return d * lax.rsqrt(var + LN_EPS) * g + b


def _params(n_grid):
    return pltpu.CompilerParams(dimension_semantics=("arbitrary",) * n_grid,
                                vmem_limit_bytes=VMEM_LIMIT)


def _qkv_kernel(x_ref, w_ref, cos_ref, sina_ref, sinb_ref, o_ref, *, rope_lo, rope_hi):
    xb = _bf16(x_ref[...])
    reps = PROJ_TN // LANES
    wide = lambda r: jnp.concatenate([r[...]] * reps, axis=1)
    for j in range(w_ref.shape[1] // PROJ_TN):
        cols = slice(j * PROJ_TN, (j + 1) * PROJ_TN)
        acc = _dot(xb, w_ref[:, cols])
        if rope_lo <= j < rope_hi:
            acc = (acc * wide(cos_ref)
                   + pltpu.roll(acc, ROPE_DIM // 2, 1) * wide(sina_ref)
                   + pltpu.roll(acc, PROJ_TN - ROPE_DIM // 2, 1) * wide(sinb_ref))
        o_ref[:, cols] = _bf16(acc)


def _qkv_proj(x2, w_in, cos_t, sina_t, sinb_t, seq, tm):
    t, d = x2.shape
    n = QKV_COLS
    s_tiles = seq // tm
    tab = pl.BlockSpec((tm, LANES), lambda i: (i % s_tiles, 0))
    rope_lo = 3 * SB_WIDTH // PROJ_TN
    rope_hi = (3 * SB_WIDTH + 2 * MOBA_WIDTH) // PROJ_TN
    return pl.pallas_call(
        functools.partial(_qkv_kernel, rope_lo=rope_lo, rope_hi=rope_hi),
        grid=(t // tm,),
        in_specs=[pl.BlockSpec((tm, d), lambda i: (i, 0)),
                  pl.BlockSpec((d, n), lambda i: (0, 0), pipeline_mode=pl.Buffered(1)),
                  tab, tab, tab],
        out_specs=pl.BlockSpec((tm, n), lambda i: (i, 0)),
        out_shape=jax.ShapeDtypeStruct((t, n), jnp.bfloat16),
        compiler_params=_params(1),
        name="qkv_proj",
    )(x2, w_in, cos_t, sina_t, sinb_t)


def _sb_kernel(q_ref, k_ref, v_ref, tri_ref, o_ref):
    i = pl.program_id(2)
    tq, win, step = SB_TQ, SB_WINDOW, SB_STEP
    lane = lax.broadcasted_iota(jnp.int32, (tq, LANES), 1)
    row = lax.broadcasted_iota(jnp.int32, (tq, win), 0)
    col = lax.broadcasted_iota(jnp.int32, (tq, win), 1)

    def visit(jobs):
        n = range(len(jobs))
        ks = [k_ref[pl.ds(start, width), :] for (_, start, width, _, _) in jobs]
        zs = [_dot_nt(jobs[u][0], ks[u]) for u in n]
        log_keep, log_beta = [], []
        for u in n:
            past = jobs[u][4]
            lb = jnp.minimum(zs[u], 0.0) - jnp.log(1.0 + jnp.exp(-jnp.abs(zs[u])))
            lk = lb - zs[u]
            log_keep.append(lk if past is None else jnp.where(past, lk, 0.0))
            log_beta.append(lb)
        suffix = [_split_dot(log_keep[u], tri_ref, jobs[u][2]) for u in n]
        out = []
        for u in n:
            _, start, width, (carry, acc), past = jobs[u]
            w = jnp.exp(log_beta[u] + suffix[u] + carry)
            w = _bf16(w if past is None else jnp.where(past, w, 0.0))
            out.append((carry + suffix[u][:, :1] + log_keep[u][:, :1],
                        acc + _dot(w, v_ref[pl.ds(start, width), :])))
        return out

    zero = (jnp.zeros((tq, 1), jnp.float32), jnp.zeros((tq, LANES), jnp.float32))
    jobs, starts, queries = [], [], []
    for t in range(SB_TILES):
        q = q_ref[t * tq:(t + 1) * tq, :]
        first_q = (i * SB_TILES + t) * tq
        start0 = pl.multiple_of(jnp.maximum(first_q + tq - win, 0), step)
        past = col - row < first_q - start0
        qs = [jnp.where(lane // HEAD_DIM == h, q, jnp.zeros_like(q)) * SCALE for h in range(HEADS_PER_TILE)]
        jobs += [(qs[h], start0, win, zero, past) for h in range(HEADS_PER_TILE)]
        starts.append(start0)
        queries.append(qs)
    state = visit(jobs)

    def worst_bound(state, k):
        parts = []
        for t in range(SB_TILES):
            both = jnp.maximum(state[t * HEADS_PER_TILE][0], state[t * HEADS_PER_TILE + 1][0])
            parts.append(jnp.where(starts[t] - k * step > 0, both, NEG_BIG))
        return jnp.max(functools.reduce(jnp.maximum, parts))

    def cond(loop):
        return loop[1] > SB_UNDERFLOW

    def body(loop):
        k, _, state = loop
        jobs = []
        for t in range(SB_TILES):
            left = starts[t] - k * step
            start = pl.multiple_of(jnp.maximum(left - step, 0), step)
            live = jnp.broadcast_to(left > 0, (tq, step))
            jobs += [(queries[t][h], start, step, state[t * HEADS_PER_TILE + h], live) for h in range(HEADS_PER_TILE)]
        state = tuple(visit(jobs))
        return k + 1, worst_bound(state, k + 1), state

    _, _, state = lax.while_loop(cond, body, (0, worst_bound(state, 0), tuple(state)))
    for t in range(SB_TILES):
        o_ref[t * tq:(t + 1) * tq, :] = _bf16(jnp.where(lane < HEAD_DIM, state[t * HEADS_PER_TILE][1],
                                                        state[t * HEADS_PER_TILE + 1][1]))


def _sb_attention(qkv, batch, seq):
    t = qkv.shape[0]
    rows = SB_TQ * SB_TILES
    assert seq % rows == 0 and seq >= SB_WINDOW
    nq = seq // rows
    hp = SB_WIDTH // LANES
    idx = jnp.arange(SB_WINDOW)
    tri = _bf16(idx[:, None] > idx[None, :])
    tri2 = jnp.concatenate([tri, tri], axis=0)
    return pl.pallas_call(
        _sb_kernel,
        grid=(batch, hp, nq),
        in_specs=[pl.BlockSpec((rows, LANES), lambda b, p, i: (b * nq + i, p)),
                  pl.BlockSpec((seq, LANES), lambda b, p, i: (b, hp + p)),
                  pl.BlockSpec((seq, LANES), lambda b, p, i: (b, 2 * hp + p)),
                  pl.BlockSpec((2 * SB_WINDOW, SB_WINDOW), lambda b, p, i: (0, 0))],
        out_specs=pl.BlockSpec((rows, LANES), lambda b, p, i: (b * nq + i, p)),
        out_shape=jax.ShapeDtypeStruct((t, SB_WIDTH), jnp.bfloat16),
        compiler_params=_params(3),
        name="sb_attention",
    )(qkv, qkv, qkv, tri2)


def _moba_kernel(q_ref, k_ref, v_ref, o_ref, kmean_ref, knorm_ref, kaug_ref, vt_ref, *, n_blocks):
    i = pl.program_id(2)
    bs = MOBA_BLOCK
    hd = HEAD_DIM
    grp = MOBA_GROUP
    heads = range(MOBA_PAIRS * HEADS_PER_TILE)
    half_of = lambda h: h % HEADS_PER_TILE
    lane = lax.broadcasted_iota(jnp.int32, (bs, LANES), 1)

    @pl.when(i == 0)
    def _():
        kmean_ref[...] = jnp.zeros_like(kmean_ref)
        r_i = lax.broadcasted_iota(jnp.int32, (8 * HEADS_PER_TILE, LANES), 0)
        l_i = lax.broadcasted_iota(jnp.int32, (8 * HEADS_PER_TILE, LANES), 1)
        head_lanes = _bf16(r_i // 8 == l_i // hd)
        norm2 = [jnp.zeros((8 * HEADS_PER_TILE, bs), jnp.float32)] * MOBA_PAIRS
        for n in range(n_blocks):
            kbs = [k_ref[n * bs:(n + 1) * bs, pr * LANES:(pr + 1) * LANES].astype(jnp.float32)
                   for pr in range(MOBA_PAIRS)]
            for pr in range(MOBA_PAIRS):
                kmean_ref[pr, hd + n:hd + n + 1, :] = jnp.sum(kbs[pr], axis=0, keepdims=True) * (1.0 / bs)
                norm2[pr] = jnp.maximum(norm2[pr], _dot_nt(head_lanes, _bf16(kbs[pr] * kbs[pr])))
            tag = (lane == hd + n).astype(jnp.float32)
            for h in heads:
                kb = kbs[h // HEADS_PER_TILE]
                dims = kb if half_of(h) == 0 else pltpu.roll(kb, LANES - half_of(h) * hd, 1)
                kaug_ref[h, n * bs:(n + 1) * bs, :] = _bf16(jnp.where(lane < hd, dims, tag))
            vt_ref[n] = _bf16(v_ref[n * bs:(n + 1) * bs, :].astype(jnp.float32).T)
        for h in heads:
            per_key = norm2[h // HEADS_PER_TILE][8 * half_of(h):8 * half_of(h) + 1]
            knorm_ref[h] = jnp.broadcast_to(jnp.sqrt(jnp.max(per_key, axis=1, keepdims=True)), (1, LANES))

    qs_in = [q_ref[:, pr * LANES:(pr + 1) * LANES] for pr in range(MOBA_PAIRS)]
    qfs = [qq.astype(jnp.float32) for qq in qs_in]
    key = lax.broadcasted_iota(jnp.int32, (bs, bs), 0)
    qry = lax.broadcasted_iota(jnp.int32, (bs, bs), 1)
    rows = lax.broadcasted_iota(jnp.int32, (MOBA_TAGS, bs), 0)
    causal = key <= qry
    kmeans = [kmean_ref[pr, hd:hd + MOBA_TAGS, :] for pr in range(MOBA_PAIRS)]
    km_his = [_bf16(km) for km in kmeans]
    km_los = [_bf16(km - hi.astype(jnp.float32)) for km, hi in zip(kmeans, km_his)]
    own_start = pl.multiple_of(i * bs, bs)
    vt_own = vt_ref[i]

    dims, sel, shift = [], [], []
    for h in heads:
        q, qf = qs_in[h // HEADS_PER_TILE], qfs[h // HEADS_PER_TILE]
        km_hi, km_lo = km_his[h // HEADS_PER_TILE], km_los[h // HEADS_PER_TILE]
        qh = jnp.where(lane // hd == half_of(h), q, jnp.zeros_like(q))
        gate = _dot_nt(km_hi, qh) + _dot_nt(km_lo, qh)
        gate = jnp.where(rows < i, gate, -jnp.inf)
        chosen = jnp.zeros((MOBA_TAGS, bs), jnp.float32)
        for _ in range(MOBA_TOPK):
            best = jnp.max(gate, axis=0, keepdims=True)
            first = jnp.min(jnp.where(gate == best, rows, MOBA_TAGS), axis=0, keepdims=True)
            pick = (rows == first) & (best > -jnp.inf)
            chosen = jnp.where(pick, 1.0, chosen)
            gate = jnp.where(pick, -jnp.inf, gate)
        pad = jnp.zeros((hd, bs), jnp.float32)
        by_lane = jnp.concatenate([pad, chosen, jnp.zeros((LANES - hd - MOBA_TAGS, bs), jnp.float32)], axis=0).T
        sel.append(by_lane > 0.0)
        d = (qf if half_of(h) == 0 else pltpu.roll(qf, LANES - half_of(h) * hd, 1)) * SCALE
        dims.append(d)
        norm = jnp.sqrt(jnp.sum(jnp.where(lane < hd, d * d, 0.0), axis=1, keepdims=True))
        shift.append(norm * knorm_ref[h] * MOBA_NORM_SLACK)

    def finish(state):
        out_t = jnp.concatenate([acc / l for (l, acc) in state], axis=0)
        return out_t.T

    def walk(q_past, step, state):
        def scores(blk):
            start = pl.multiple_of(jnp.minimum(blk, n_blocks - 1) * bs, bs)
            return tuple(_dot_nt(kaug_ref[h, pl.ds(start, bs), :], q_past[h]) for h in heads)

        def past_group(g, carry):
            state, s_next = carry
            state = list(state)
            for c in range(grp):
                s_cur, s_next = s_next, scores(g * grp + c + 1)
                for h in heads:
                    state[h] = step(state[h], s_cur[h], vt_ref[g * grp + c, h * hd:(h + 1) * hd, :])
            return tuple(state), s_next

        state, _ = lax.fori_loop(0, (i + grp - 1) // grp, past_group, (tuple(state), scores(0)))
        return state

    def shifted():
        q_past, state = [], []
        for h in heads:
            q_past.append(_bf16(jnp.where(lane < hd, dims[h], jnp.where(sel[h], -shift[h], NEG_BIG))))
            q_own = _bf16(jnp.where(lane < hd, dims[h], jnp.where(lane == hd + i, -shift[h], 0.0)))
            s = _dot_nt(kaug_ref[h, pl.ds(own_start, bs), :], q_own)
            p = jnp.where(causal, jnp.exp(s), 0.0)
            state.append((jnp.sum(p, axis=0, keepdims=True), _dot(vt_own[h * hd:(h + 1) * hd, :], _bf16(p))))

        def step(st, s, vt):
            p = jnp.exp(s)
            return st[0] + jnp.sum(p, axis=0, keepdims=True), st[1] + _dot(vt, _bf16(p))

        return finish(walk(q_past, step, state))

    def running_max():
        q_past, state = [], []
        for h in heads:
            q_past.append(_bf16(jnp.where(lane < hd, dims[h], jnp.where(sel[h], 0.0, NEG_BIG))))
            q_own = _bf16(jnp.where(lane < hd, dims[h], 0.0))
            s = jnp.where(causal, _dot_nt(kaug_ref[h, pl.ds(own_start, bs), :], q_own), NEG_BIG)
            m = jnp.max(s, axis=0, keepdims=True)
            p = jnp.exp(s - m)
            state.append((m, jnp.sum(p, axis=0, keepdims=True), _dot(vt_own[h * hd:(h + 1) * hd, :], _bf16(p))))

        def step(st, s, vt):
            m_old, l_old, acc_old = st
            m_new = jnp.maximum(m_old, jnp.max(s, axis=0, keepdims=True))
            p = jnp.exp(s - m_new)
            alpha = jnp.exp(m_old - m_new)
            return m_new, alpha * l_old + jnp.sum(p, axis=0, keepdims=True), alpha * acc_old + _dot(vt, _bf16(p))

        return finish([st[1:] for st in walk(q_past, step, state)])

    largest = jnp.max(functools.reduce(jnp.maximum, shift))
    o_ref[...] = _bf16(lax.cond(2.0 * largest <= MOBA_SHIFT_MAX, shifted, running_max))


def _moba_attention(qkv, batch, seq):
    t = qkv.shape[0]
    nq = seq // MOBA_BLOCK
    assert nq % MOBA_GROUP == 0, "the grouped block walk reads whole groups"
    assert nq <= MOBA_TAGS, "one block tag lane per key block"
    width = MOBA_PAIRS * LANES
    hp = MOBA_WIDTH // width
    base = 3 * SB_WIDTH // width
    n_heads = MOBA_PAIRS * HEADS_PER_TILE
    return pl.pallas_call(
        functools.partial(_moba_kernel, n_blocks=nq),
        grid=(batch, hp, nq),
        in_specs=[pl.BlockSpec((MOBA_BLOCK, width), lambda b, p, i: (b * nq + i, base + p)),
                  pl.BlockSpec((seq, width), lambda b, p, i: (b, base + hp + p)),
                  pl.BlockSpec((seq, width), lambda b, p, i: (b, base + 2 * hp + p))],
        out_specs=pl.BlockSpec((MOBA_BLOCK, width), lambda b, p, i: (b * nq + i, p)),
        out_shape=jax.ShapeDtypeStruct((t, MOBA_WIDTH), jnp.bfloat16),
        scratch_shapes=[pltpu.VMEM((MOBA_PAIRS, LANES, LANES), jnp.float32),
                        pltpu.VMEM((n_heads, 1, LANES), jnp.float32),
                        pltpu.VMEM((n_heads, seq, LANES), jnp.bfloat16),
                        pltpu.VMEM((nq, width, MOBA_BLOCK), jnp.bfloat16)],
        compiler_params=_params(3),
        name="moba_attention",
    )(qkv, qkv, qkv)


def _mix_kernel(x_ref, osb_ref, omb_ref, wgs_ref, wgm_ref, wbs_ref, wbm_ref, wo_ref, g_ref, b_ref, o_ref, *, alpha):
    x = x_ref[...]
    xb = _bf16(x)
    branch_sb = _dot(osb_ref[...], wbs_ref[...])
    branch_mb = _dot(omb_ref[...], wbm_ref[...])
    merged = (jax.nn.sigmoid(_dot(xb, wgs_ref[...])) * branch_sb
              + jax.nn.sigmoid(_dot(xb, wgm_ref[...])) * branch_mb)
    mix = _dot(_bf16(merged), wo_ref[...])
    o_ref[...] = _layer_norm(alpha * x + mix, g_ref[...], b_ref[...])


def _mix_block(x2, o_sb, o_mb, w_in, w_bsb, w_bmb, w_out, ln_g, ln_b, alpha, tm):
    t, d = x2.shape
    assert QKV_COLS % d == 0
    rows = lambda w: pl.BlockSpec((tm, w), lambda i: (i, 0))
    whole = lambda a: pl.BlockSpec(a.shape, lambda i: (0, 0), pipeline_mode=pl.Buffered(1))
    gate_cols = lambda j: pl.BlockSpec((d, d), lambda i: (0, QKV_COLS // d + j), pipeline_mode=pl.Buffered(1))
    return pl.pallas_call(
        functools.partial(_mix_kernel, alpha=alpha),
        grid=(t // tm,),
        in_specs=[rows(d), rows(SB_WIDTH), rows(MOBA_WIDTH),
                  gate_cols(0), gate_cols(1), whole(w_bsb), whole(w_bmb), whole(w_out), whole(ln_g), whole(ln_b)],
        out_specs=rows(d),
        out_shape=jax.ShapeDtypeStruct((t, d), jnp.float32),
        compiler_params=_params(1),
        name="mix_ln",
    )(x2, o_sb, o_mb, w_in, w_in, w_bsb, w_bmb, w_out, ln_g, ln_b)


def _ffn_kernel(x_ref, wg_ref, wu_ref, wd_ref, g_ref, b_ref, o_ref, *, alpha, chunk):
    x = x_ref[...]
    xb = _bf16(x)
    acc = alpha * x
    for c in range(0, wg_ref.shape[1], chunk):
        gate = _dot(xb, wg_ref[:, c:c + chunk])
        up = _dot(xb, wu_ref[:, c:c + chunk])
        acc = acc + _dot(_bf16(jax.nn.silu(gate) * up), wd_ref[c:c + chunk, :])
    o_ref[...] = _layer_norm(acc, g_ref[...], b_ref[...])


def _ffn_chunk(d_ff):
    best = LANES
    for c in range(LANES, 1536 + 1, LANES):
        if d_ff % c == 0:
            best = c
    return best


def _ffn_block(x2, w_gate, w_up, w_down, ln_g, ln_b, alpha, tm):
    t, d = x2.shape
    rows = pl.BlockSpec((tm, d), lambda i: (i, 0))
    whole = lambda a: pl.BlockSpec(a.shape, lambda i: (0, 0), pipeline_mode=pl.Buffered(1))
    return pl.pallas_call(
        functools.partial(_ffn_kernel, alpha=alpha, chunk=_ffn_chunk(w_gate.shape[1])),
        grid=(t // tm,),
        in_specs=[rows, whole(w_gate), whole(w_up), whole(w_down), whole(ln_g), whole(ln_b)],
        out_specs=rows,
        out_shape=jax.ShapeDtypeStruct((t, d), jnp.float32),
        compiler_params=_params(1),
        name="ffn_ln",
    )(x2, w_gate, w_up, w_down, ln_g, ln_b)


def _rope_tables(seq):
    half = ROPE_DIM // 2
    inv_freq = ROPE_THETA ** (-jnp.arange(0, ROPE_DIM, 2, dtype=jnp.float32) / ROPE_DIM)
    dim = np.arange(LANES) % HEAD_DIM
    ang = jnp.arange(seq).astype(jnp.float32)[:, None] * inv_freq[dim % half][None, :]
    cos, sin = jnp.cos(ang), jnp.sin(ang)
    rotary = dim < ROPE_DIM
    return (jnp.where(rotary, cos, 1.0),
            jnp.where(rotary & (dim >= half), sin, 0.0),
            jnp.where(dim < half, -sin, 0.0))


def kernel(x, w_in, w_branch_sb, w_branch_moba, w_out, ln_mix_g, ln_mix_b,
           w_ffn_gate, w_ffn_up, w_ffn_down, ln_ffn_g, ln_ffn_b):
    batch, seq, d = x.shape
    depth = w_in.shape[0]
    assert w_in.shape[2] == QKV_COLS + 2 * d
    assert seq % MOBA_BLOCK == 0
    alpha = (2 * depth) ** 0.25
    tm = min(512, seq)
    cos_t, sina_t, sinb_t = _rope_tables(seq)
    x2 = x.reshape(batch * seq, d)
    for l in range(depth):
        w_l = _bf16(w_in[l])
        qkv = _qkv_proj(x2, w_l, cos_t, sina_t, sinb_t, seq, tm)
        o_sb = _sb_attention(qkv, batch, seq)
        o_mb = _moba_attention(qkv, batch, seq)
        x2 = _mix_block(x2, o_sb, o_mb, w_l, _bf16(w_branch_sb[l]), _bf16(w_branch_moba[l]),
                        _bf16(w_out[l]), ln_mix_g[l][None, :], ln_mix_b[l][None, :], alpha, tm)
        x2 = _ffn_block(x2, _bf16(w_ffn_gate[l]), _bf16(w_ffn_up[l]), _bf16(w_ffn_down[l]),
                        ln_ffn_g[l][None, :], ln_ffn_b[l][None, :], alpha, tm)
    return x2.reshape(batch, seq, d)
```

```python
import functools

import jax
import jax.numpy as jnp
import numpy as np
from jax import lax
from jax.experimental import pallas as pl
from jax.experimental.pallas import tpu as pltpu

HEAD_DIM = 64
SB_HEADS = 8
MOBA_HEADS = 8
SB_WIDTH = SB_HEADS * HEAD_DIM
MOBA_WIDTH = MOBA_HEADS * HEAD_DIM
QKV_COLS = 3 * SB_WIDTH + 3 * MOBA_WIDTH
MOBA_BLOCK = 256
MOBA_TOPK = 3
ROPE_THETA = 500000.0
ROPE_DIM = HEAD_DIM // 4
LN_EPS = 1e-5
SCALE = HEAD_DIM ** -0.5

LANES = 128
HEADS_PER_TILE = LANES // HEAD_DIM
VMEM_LIMIT = 56 * 1024 * 1024
PROJ_TN = 512
SB_TQ = 64
SB_WINDOW = 256
SB_STEP = 64
MOBA_GROUP = 4
ATTN_PAIRS = 2
MOBA_TAGS = 32
MOBA_SHIFT_MAX = 60.0
MOBA_NORM_SLACK = 1.01
NEG_BIG = -1e30
SB_UNDERFLOW = -104.0

_NT = (((1,), (1,)), ((), ()))


def _bf16(a):
    return a.astype(jnp.bfloat16)


def _dot(a, b):
    return jnp.dot(a, b, preferred_element_type=jnp.float32)


def _dot_nt(a, b):
    return lax.dot_general(a, b, _NT, preferred_element_type=jnp.float32)


def _split_dot(a_f32, tri2_ref, width):
    hi = _bf16(a_f32)
    lo = _bf16(a_f32 - hi.astype(jnp.float32))
    if width == SB_WINDOW:
        return _dot(jnp.concatenate([hi, lo], axis=1), tri2_ref[...])
    tri = tri2_ref[:width, :width]
    return _dot(jnp.concatenate([hi, lo], axis=1), jnp.concatenate([tri, tri], axis=0))


def _layer_norm(y, g, b):
    mu = jnp.mean(y, axis=-1, keepdims=True)
    d = y - mu
    var = jnp.mean(d * d, axis=-1, keepdims=True)
    return d * lax.rsqrt(var + LN_EPS) * g + b


def _params(n_grid):
    return pltpu.CompilerParams(dimension_semantics=("arbitrary",) * n_grid,
                                vmem_limit_bytes=VMEM_LIMIT)


def _qkv_kernel(x_ref, w_ref, cos_ref, sina_ref, sinb_ref, o_ref, *, rope_lo, rope_hi):
    xb = _bf16(x_ref[...])
    reps = PROJ_TN // LANES
    wide = lambda r: jnp.concatenate([r[...]] * reps, axis=1)
    for j in range(w_ref.shape[1] // PROJ_TN):
        cols = slice(j * PROJ_TN, (j + 1) * PROJ_TN)
        acc = _dot(xb, w_ref[:, cols])
        if rope_lo <= j < rope_hi:
            acc = (acc * wide(cos_ref)
                   + pltpu.roll(acc, ROPE_DIM // 2, 1) * wide(sina_ref)
                   + pltpu.roll(acc, PROJ_TN - ROPE_DIM // 2, 1) * wide(sinb_ref))
        o_ref[:, cols] = _bf16(acc)


def _qkv_proj(x2, w_in, cos_t, sina_t, sinb_t, seq, tm):
    t, d = x2.shape
    n = QKV_COLS
    s_tiles = seq // tm
    tab = pl.BlockSpec((tm, LANES), lambda i: (i % s_tiles, 0))
    rope_lo = 3 * SB_WIDTH // PROJ_TN
    rope_hi = (3 * SB_WIDTH + 2 * MOBA_WIDTH) // PROJ_TN
    return pl.pallas_call(
        functools.partial(_qkv_kernel, rope_lo=rope_lo, rope_hi=rope_hi),
        grid=(t // tm,),
        in_specs=[pl.BlockSpec((tm, d), lambda i: (i, 0)),
                  pl.BlockSpec((d, n), lambda i: (0, 0), pipeline_mode=pl.Buffered(1)),
                  tab, tab, tab],
        out_specs=pl.BlockSpec((tm, n), lambda i: (i, 0)),
        out_shape=jax.ShapeDtypeStruct((t, n), jnp.bfloat16),
        compiler_params=_params(1),
        name="qkv_proj",
    )(x2, w_in, cos_t, sina_t, sinb_t)


def _sb_scores(k_ref, jobs):
    return [_dot_nt(q, k_ref[pl.ds(start, width), tile * LANES:(tile + 1) * LANES])
            for (q, tile, start, width, _, _) in jobs]


def _sb_logs(tri_ref, jobs, zs):
    log_keep, log_beta = [], []
    for (_, _, _, _, _, mask), z in zip(jobs, zs):
        lb = jnp.minimum(z, 0.0) - jnp.log(1.0 + jnp.exp(-jnp.abs(z)))
        lk = lb - z
        log_keep.append(lk if mask is None else jnp.where(mask, lk, 0.0))
        log_beta.append(lb)
    suffix = [_split_dot(lk, tri_ref, job[3]) for job, lk in zip(jobs, log_keep)]
    return log_keep, log_beta, suffix


def _sb_weights(v_ref, jobs, logs):
    log_keep, log_beta, suffix = logs
    out = []
    for u, (_, tile, start, width, (carry, acc), mask) in enumerate(jobs):
        w = jnp.exp(log_beta[u] + suffix[u] + carry)
        w = _bf16(w if mask is None else jnp.where(mask, w, 0.0))
        out.append((carry + suffix[u][:, :1] + log_keep[u][:, :1],
                    acc + _dot(w, v_ref[pl.ds(start, width), tile * LANES:(tile + 1) * LANES])))
    return out


def _attn_kernel(qs_ref, ks_ref, vs_ref, tri_ref, q_ref, k_ref, v_ref, os_ref, o_ref,
                 kmean_ref, knorm_ref, kaug_ref, vt_ref, *, n_blocks):
    i = pl.program_id(2)
    bs = MOBA_BLOCK
    hd = HEAD_DIM
    grp = MOBA_GROUP
    heads = range(ATTN_PAIRS * HEADS_PER_TILE)
    half_of = lambda h: h % HEADS_PER_TILE
    lane = lax.broadcasted_iota(jnp.int32, (bs, LANES), 1)

    tq, win, step = SB_TQ, SB_WINDOW, SB_STEP
    n_tiles = bs // tq
    sb_lane = lax.broadcasted_iota(jnp.int32, (tq, LANES), 1)
    sb_row = lax.broadcasted_iota(jnp.int32, (tq, win), 0)
    sb_col = lax.broadcasted_iota(jnp.int32, (tq, win), 1)
    zero = (jnp.zeros((tq, 1), jnp.float32), jnp.zeros((tq, LANES), jnp.float32))
    sb_jobs, sb_starts, sb_q = [], [], []
    for t in range(n_tiles):
        first_q = i * bs + t * tq
        start0 = pl.multiple_of(jnp.maximum(first_q + tq - win, 0), step)
        past = sb_col - sb_row < first_q - start0
        sb_starts.append(start0)
        for pr in range(ATTN_PAIRS):
            q = qs_ref[t * tq:(t + 1) * tq, pr * LANES:(pr + 1) * LANES]
            for h in range(HEADS_PER_TILE):
                qh = jnp.where(sb_lane // hd == h, q, jnp.zeros_like(q)) * SCALE
                sb_q.append(qh)
                sb_jobs.append((qh, pr, start0, win, zero, past))
    per_tile = ATTN_PAIRS * HEADS_PER_TILE

    @pl.when(i == 0)
    def _():
        kmean_ref[...] = jnp.zeros_like(kmean_ref)
        r_i = lax.broadcasted_iota(jnp.int32, (8 * HEADS_PER_TILE, LANES), 0)
        l_i = lax.broadcasted_iota(jnp.int32, (8 * HEADS_PER_TILE, LANES), 1)
        head_lanes = _bf16(r_i // 8 == l_i // hd)
        norm2 = [jnp.zeros((8 * HEADS_PER_TILE, bs), jnp.float32)] * ATTN_PAIRS
        for n in range(n_blocks):
            kbs = [k_ref[n * bs:(n + 1) * bs, pr * LANES:(pr + 1) * LANES].astype(jnp.float32)
                   for pr in range(ATTN_PAIRS)]
            for pr in range(ATTN_PAIRS):
                kmean_ref[pr, hd + n:hd + n + 1, :] = jnp.sum(kbs[pr], axis=0, keepdims=True) * (1.0 / bs)
                norm2[pr] = jnp.maximum(norm2[pr], _dot_nt(head_lanes, _bf16(kbs[pr] * kbs[pr])))
            tag = (lane == hd + n).astype(jnp.float32)
            for h in heads:
                kb = kbs[h // HEADS_PER_TILE]
                dims = kb if half_of(h) == 0 else pltpu.roll(kb, LANES - half_of(h) * hd, 1)
                kaug_ref[h, n * bs:(n + 1) * bs, :] = _bf16(jnp.where(lane < hd, dims, tag))
            vt_ref[n] = _bf16(v_ref[n * bs:(n + 1) * bs, :].astype(jnp.float32).T)
        for h in heads:
            per_key = norm2[h // HEADS_PER_TILE][8 * half_of(h):8 * half_of(h) + 1]
            knorm_ref[h] = jnp.broadcast_to(jnp.sqrt(jnp.max(per_key, axis=1, keepdims=True)), (1, LANES))

    qs_in = [q_ref[:, pr * LANES:(pr + 1) * LANES] for pr in range(ATTN_PAIRS)]
    qfs = [qq.astype(jnp.float32) for qq in qs_in]
    key = lax.broadcasted_iota(jnp.int32, (bs, bs), 0)
    qry = lax.broadcasted_iota(jnp.int32, (bs, bs), 1)
    rows = lax.broadcasted_iota(jnp.int32, (MOBA_TAGS, bs), 0)
    causal = key <= qry
    kmeans = [kmean_ref[pr, hd:hd + MOBA_TAGS, :] for pr in range(ATTN_PAIRS)]
    km_his = [_bf16(km) for km in kmeans]
    km_los = [_bf16(km - hi.astype(jnp.float32)) for km, hi in zip(kmeans, km_his)]
    own_start = pl.multiple_of(i * bs, bs)
    vt_own = vt_ref[i]

    dims, sel, shift = [], [], []
    for h in heads:
        q, qf = qs_in[h // HEADS_PER_TILE], qfs[h // HEADS_PER_TILE]
        km_hi, km_lo = km_his[h // HEADS_PER_TILE], km_los[h // HEADS_PER_TILE]
        qh = jnp.where(lane // hd == half_of(h), q, jnp.zeros_like(q))
        gate = _dot_nt(km_hi, qh) + _dot_nt(km_lo, qh)
        gate = jnp.where(rows < i, gate, -jnp.inf)
        chosen = jnp.zeros((MOBA_TAGS, bs), jnp.float32)
        for _ in range(MOBA_TOPK):
            best = jnp.max(gate, axis=0, keepdims=True)
            first = jnp.min(jnp.where(gate == best, rows, MOBA_TAGS), axis=0, keepdims=True)
            pick = (rows == first) & (best > -jnp.inf)
            chosen = jnp.where(pick, 1.0, chosen)
            gate = jnp.where(pick, -jnp.inf, gate)
        pad = jnp.zeros((hd, bs), jnp.float32)
        by_lane = jnp.concatenate([pad, chosen, jnp.zeros((LANES - hd - MOBA_TAGS, bs), jnp.float32)], axis=0).T
        sel.append(by_lane > 0.0)
        d = (qf if half_of(h) == 0 else pltpu.roll(qf, LANES - half_of(h) * hd, 1)) * SCALE
        dims.append(d)
        norm = jnp.sqrt(jnp.sum(jnp.where(lane < hd, d * d, 0.0), axis=1, keepdims=True))
        shift.append(norm * knorm_ref[h] * MOBA_NORM_SLACK)

    def finish(state):
        out_t = jnp.concatenate([acc / l for (l, acc) in state], axis=0)
        return out_t.T

    def walk(q_past, step_fn, state):
        def scores(blk):
            start = pl.multiple_of(jnp.minimum(blk, n_blocks - 1) * bs, bs)
            return tuple(_dot_nt(kaug_ref[h, pl.ds(start, bs), :], q_past[h]) for h in heads)

        def past_group(g, carry):
            state, s_next = carry
            state = list(state)
            for c in range(grp):
                s_cur, s_next = s_next, scores(g * grp + c + 1)
                for h in heads:
                    state[h] = step_fn(state[h], s_cur[h], vt_ref[g * grp + c, h * hd:(h + 1) * hd, :])
            return tuple(state), s_next

        state, _ = lax.fori_loop(0, (i + grp - 1) // grp, past_group, (tuple(state), scores(0)))
        return state

    def shifted():
        sb_z = _sb_scores(ks_ref, sb_jobs)
        q_past = [_bf16(jnp.where(lane < hd, dims[h], jnp.where(sel[h], -shift[h], NEG_BIG))) for h in heads]
        own = [_dot_nt(kaug_ref[h, pl.ds(own_start, bs), :],
                       _bf16(jnp.where(lane < hd, dims[h], jnp.where(lane == hd + i, -shift[h], 0.0))))
               for h in heads]
        sb_logs = _sb_logs(tri_ref, sb_jobs, sb_z)
        state = []
        for h in heads:
            p = jnp.where(causal, jnp.exp(own[h]), 0.0)
            state.append((jnp.sum(p, axis=0, keepdims=True), _dot(vt_own[h * hd:(h + 1) * hd, :], _bf16(p))))
        sb_state = _sb_weights(vs_ref, sb_jobs, sb_logs)

        def step_fn(st, s, vt):
            p = jnp.exp(s)
            return st[0] + jnp.sum(p, axis=0, keepdims=True), st[1] + _dot(vt, _bf16(p))

        return finish(walk(q_past, step_fn, state)), tuple(sb_state)

    def running_max():
        sb_state = _sb_weights(vs_ref, sb_jobs, _sb_logs(tri_ref, sb_jobs, _sb_scores(ks_ref, sb_jobs)))
        q_past, state = [], []
        for h in heads:
            q_past.append(_bf16(jnp.where(lane < hd, dims[h], jnp.where(sel[h], 0.0, NEG_BIG))))
            q_own = _bf16(jnp.where(lane < hd, dims[h], 0.0))
            s = jnp.where(causal, _dot_nt(kaug_ref[h, pl.ds(own_start, bs), :], q_own), NEG_BIG)
            m = jnp.max(s, axis=0, keepdims=True)
            p = jnp.exp(s - m)
            state.append((m, jnp.sum(p, axis=0, keepdims=True), _dot(vt_own[h * hd:(h + 1) * hd, :], _bf16(p))))

        def step_fn(st, s, vt):
            m_old, l_old, acc_old = st
            m_new = jnp.maximum(m_old, jnp.max(s, axis=0, keepdims=True))
            p = jnp.exp(s - m_new)
            alpha = jnp.exp(m_old - m_new)
            return m_new, alpha * l_old + jnp.sum(p, axis=0, keepdims=True), alpha * acc_old + _dot(vt, _bf16(p))

        return finish([st[1:] for st in walk(q_past, step_fn, state)]), tuple(sb_state)

    largest = jnp.max(functools.reduce(jnp.maximum, shift))
    moba_out, sb_state = lax.cond(2.0 * largest <= MOBA_SHIFT_MAX, shifted, running_max)
    o_ref[...] = _bf16(moba_out)

    def worst_bound(state, k):
        parts = []
        for t in range(n_tiles):
            both = functools.reduce(jnp.maximum, [state[t * per_tile + u][0] for u in range(per_tile)])
            parts.append(jnp.where(sb_starts[t] - k * step > 0, both, NEG_BIG))
        return jnp.max(functools.reduce(jnp.maximum, parts))

    def cond(loop):
        return loop[1] > SB_UNDERFLOW

    def body(loop):
        k, _, state = loop
        jobs = []
        for t in range(n_tiles):
            left = sb_starts[t] - k * step
            start = pl.multiple_of(jnp.maximum(left - step, 0), step)
            live = jnp.broadcast_to(left > 0, (tq, step))
            for u in range(per_tile):
                job = t * per_tile + u
                jobs.append((sb_q[job], sb_jobs[job][1], start, step, state[job], live))
        state = tuple(_sb_weights(vs_ref, jobs, _sb_logs(tri_ref, jobs, _sb_scores(ks_ref, jobs))))
        return k + 1, worst_bound(state, k + 1), state

    _, _, sb_state = lax.while_loop(cond, body, (0, worst_bound(sb_state, 0), sb_state))
    for t in range(n_tiles):
        for pr in range(ATTN_PAIRS):
            job = t * per_tile + pr * HEADS_PER_TILE
            os_ref[t * tq:(t + 1) * tq, pr * LANES:(pr + 1) * LANES] = _bf16(
                jnp.where(sb_lane < hd, sb_state[job][1], sb_state[job + 1][1]))


def _attention(qkv, batch, seq):
    t = qkv.shape[0]
    nq = seq // MOBA_BLOCK
    assert nq % MOBA_GROUP == 0, "the grouped block walk reads whole groups"
    assert nq <= MOBA_TAGS, "one block tag lane per key block"
    assert MOBA_BLOCK % SB_TQ == 0 and seq >= SB_WINDOW and SB_WIDTH == MOBA_WIDTH
    width = ATTN_PAIRS * LANES
    hp = MOBA_WIDTH // width
    n_heads = ATTN_PAIRS * HEADS_PER_TILE
    idx = jnp.arange(SB_WINDOW)
    tri = _bf16(idx[:, None] > idx[None, :])
    tri2 = jnp.concatenate([tri, tri], axis=0)
    rows_blk = lambda sec: pl.BlockSpec((MOBA_BLOCK, width), lambda b, p, i: (b * nq + i, sec * hp + p))
    seq_blk = lambda sec, bufs: pl.BlockSpec((seq, width), lambda b, p, i: (b, sec * hp + p),
                                             pipeline_mode=pl.Buffered(bufs))
    out_blk = pl.BlockSpec((MOBA_BLOCK, width), lambda b, p, i: (b * nq + i, p))
    return pl.pallas_call(
        functools.partial(_attn_kernel, n_blocks=nq),
        grid=(batch, hp, nq),
        in_specs=[rows_blk(0), seq_blk(1, 2), seq_blk(2, 2),
                  pl.BlockSpec((2 * SB_WINDOW, SB_WINDOW), lambda b, p, i: (0, 0)),
                  rows_blk(3), seq_blk(4, 1), seq_blk(5, 1)],
        out_specs=[out_blk, out_blk],
        out_shape=[jax.ShapeDtypeStruct((t, SB_WIDTH), jnp.bfloat16),
                   jax.ShapeDtypeStruct((t, MOBA_WIDTH), jnp.bfloat16)],
        scratch_shapes=[pltpu.VMEM((ATTN_PAIRS, LANES, LANES), jnp.float32),
                        pltpu.VMEM((n_heads, 1, LANES), jnp.float32),
                        pltpu.VMEM((n_heads, seq, LANES), jnp.bfloat16),
                        pltpu.VMEM((nq, width, MOBA_BLOCK), jnp.bfloat16)],
        compiler_params=_params(3),
        name="attention",
    )(qkv, qkv, qkv, tri2, qkv, qkv, qkv)


def _mix_kernel(x_ref, osb_ref, omb_ref, wgs_ref, wgm_ref, wbs_ref, wbm_ref, wo_ref, g_ref, b_ref, o_ref, *, alpha):
    x = x_ref[...]
    xb = _bf16(x)
    branch_sb = _dot(osb_ref[...], wbs_ref[...])
    branch_mb = _dot(omb_ref[...], wbm_ref[...])
    merged = (jax.nn.sigmoid(_dot(xb, wgs_ref[...])) * branch_sb
              + jax.nn.sigmoid(_dot(xb, wgm_ref[...])) * branch_mb)
    mix = _dot(_bf16(merged), wo_ref[...])
    o_ref[...] = _layer_norm(alpha * x + mix, g_ref[...], b_ref[...])


def _mix_block(x2, o_sb, o_mb, w_in, w_bsb, w_bmb, w_out, ln_g, ln_b, alpha, tm):
    t, d = x2.shape
    assert QKV_COLS % d == 0
    rows = lambda w: pl.BlockSpec((tm, w), lambda i: (i, 0))
    whole = lambda a: pl.BlockSpec(a.shape, lambda i: (0, 0), pipeline_mode=pl.Buffered(1))
    gate_cols = lambda j: pl.BlockSpec((d, d), lambda i: (0, QKV_COLS // d + j), pipeline_mode=pl.Buffered(1))
    return pl.pallas_call(
        functools.partial(_mix_kernel, alpha=alpha),
        grid=(t // tm,),
        in_specs=[rows(d), rows(SB_WIDTH), rows(MOBA_WIDTH),
                  gate_cols(0), gate_cols(1), whole(w_bsb), whole(w_bmb), whole(w_out), whole(ln_g), whole(ln_b)],
        out_specs=rows(d),
        out_shape=jax.ShapeDtypeStruct((t, d), jnp.float32),
        compiler_params=_params(1),
        name="mix_ln",
    )(x2, o_sb, o_mb, w_in, w_in, w_bsb, w_bmb, w_out, ln_g, ln_b)


def _ffn_kernel(x_ref, wg_ref, wu_ref, wd_ref, g_ref, b_ref, o_ref, *, alpha, chunk):
    x = x_ref[...]
    xb = _bf16(x)
    acc = alpha * x
    for c in range(0, wg_ref.shape[1], chunk):
        gate = _dot(xb, wg_ref[:, c:c + chunk])
        up = _dot(xb, wu_ref[:, c:c + chunk])
        acc = acc + _dot(_bf16(jax.nn.silu(gate) * up), wd_ref[c:c + chunk, :])
    o_ref[...] = _layer_norm(acc, g_ref[...], b_ref[...])


def _ffn_chunk(d_ff):
    best = LANES
    for c in range(LANES, 1536 + 1, LANES):
        if d_ff % c == 0:
            best = c
    return best


def _ffn_block(x2, w_gate, w_up, w_down, ln_g, ln_b, alpha, tm):
    t, d = x2.shape
    rows = pl.BlockSpec((tm, d), lambda i: (i, 0))
    whole = lambda a: pl.BlockSpec(a.shape, lambda i: (0, 0), pipeline_mode=pl.Buffered(1))
    return pl.pallas_call(
        functools.partial(_ffn_kernel, alpha=alpha, chunk=_ffn_chunk(w_gate.shape[1])),
        grid=(t // tm,),
        in_specs=[rows, whole(w_gate), whole(w_up), whole(w_down), whole(ln_g), whole(ln_b)],
        out_specs=rows,
        out_shape=jax.ShapeDtypeStruct((t, d), jnp.float32),
        compiler_params=_params(1),
        name="ffn_ln",
    )(x2, w_gate, w_up, w_down, ln_g, ln_b)


def _rope_tables(seq):
    half = ROPE_DIM // 2
    inv_freq = ROPE_THETA ** (-jnp.arange(0, ROPE_DIM, 2, dtype=jnp.float32) / ROPE_DIM)
    dim = np.arange(LANES) % HEAD_DIM
    ang = jnp.arange(seq).astype(jnp.float32)[:, None] * inv_freq[dim % half][None, :]
    cos, sin = jnp.cos(ang), jnp.sin(ang)
    rotary = dim < ROPE_DIM
    return (jnp.where(rotary, cos, 1.0),
            jnp.where(rotary & (dim >= half), sin, 0.0),
            jnp.where(dim < half, -sin, 0.0))


def kernel(x, w_in, w_branch_sb, w_branch_moba, w_out, ln_mix_g, ln_mix_b,
           w_ffn_gate, w_ffn_up, w_ffn_down, ln_ffn_g, ln_ffn_b):
    batch, seq, d = x.shape
    depth = w_in.shape[0]
    assert w_in.shape[2] == QKV_COLS + 2 * d
    assert seq % MOBA_BLOCK == 0
    alpha = (2 * depth) ** 0.25
    tm = min(512, seq)
    cos_t, sina_t, sinb_t = _rope_tables(seq)
    x2 = x.reshape(batch * seq, d)
    for l in range(depth):
        w_l = _bf16(w_in[l])
        qkv = _qkv_proj(x2, w_l, cos_t, sina_t, sinb_t, seq, tm)
        o_sb, o_mb = _attention(qkv, batch, seq)
        x2 = _mix_block(x2, o_sb, o_mb, w_l, _bf16(w_branch_sb[l]), _bf16(w_branch_moba[l]),
                        _bf16(w_out[l]), ln_mix_g[l][None, :], ln_mix_b[l][None, :], alpha, tm)
        x2 = _ffn_block(x2, _bf16(w_ffn_gate[l]), _bf16(w_ffn_up[l]), _bf16(w_ffn_down[l]),
                        ln_ffn_g[l][None, :], ln_ffn_b[l][None, :], alpha, tm)
    return x2.reshape(batch, seq, d)
```

```python
import functools

import jax
import jax.numpy as jnp
import numpy as np
from jax import lax
from jax.experimental import pallas as pl
from jax.experimental.pallas import tpu as pltpu

HEAD_DIM = 64
SB_HEADS = 8
MOBA_HEADS = 8
SB_WIDTH = SB_HEADS * HEAD_DIM
MOBA_WIDTH = MOBA_HEADS * HEAD_DIM
QKV_COLS = 3 * SB_WIDTH + 3 * MOBA_WIDTH
MOBA_BLOCK = 256
MOBA_TOPK = 3
ROPE_THETA = 500000.0
ROPE_DIM = HEAD_DIM // 4
LN_EPS = 1e-5
SCALE = HEAD_DIM ** -0.5

LANES = 128
HEADS_PER_TILE = LANES // HEAD_DIM
VMEM_LIMIT = 56 * 1024 * 1024
PROJ_TN = 512
SB_TQ = 64
SB_WINDOW = 256
SB_STEP = 64
MOBA_GROUP = 4
ATTN_PAIRS = 2
MOBA_TAGS = 32
MOBA_SHIFT_MAX = 60.0
MOBA_NORM_SLACK = 1.01
NEG_BIG = -1e30
SB_UNDERFLOW = -104.0

_NT = (((1,), (1,)), ((), ()))


def _bf16(a):
    return a.astype(jnp.bfloat16)


def _dot(a, b):
    return jnp.dot(a, b, preferred_element_type=jnp.float32)


def _dot_nt(a, b):
    return lax.dot_general(a, b, _NT, preferred_element_type=jnp.float32)


def _split_dot(a_f32, tri2_ref, width):
    hi = _bf16(a_f32)
    lo = _bf16(a_f32 - hi.astype(jnp.float32))
    if width == SB_WINDOW:
        return _dot(jnp.concatenate([hi, lo], axis=1), tri2_ref[...])
    tri = tri2_ref[:width, :width]
    return _dot(jnp.concatenate([hi, lo], axis=1), jnp.concatenate([tri, tri], axis=0))


def _layer_norm(y, g, b):
    mu = jnp.mean(y, axis=-1, keepdims=True)
    d = y - mu
    var = jnp.mean(d * d, axis=-1, keepdims=True)
    return d * lax.rsqrt(var + LN_EPS) * g + b


def _params(n_grid):
    return pltpu.CompilerParams(dimension_semantics=("arbitrary",) * n_grid,
                                vmem_limit_bytes=VMEM_LIMIT)


def _qkv_kernel(x_ref, w_ref, cos_ref, sina_ref, sinb_ref, o_ref, *, rope_lo, rope_hi):
    xb = _bf16(x_ref[...])
    reps = PROJ_TN // LANES
    wide = lambda r: jnp.concatenate([r[...]] * reps, axis=1)
    for j in range(w_ref.shape[1] // PROJ_TN):
        cols = slice(j * PROJ_TN, (j + 1) * PROJ_TN)
        acc = _dot(xb, w_ref[:, cols])
        if rope_lo <= j < rope_hi:
            acc = (acc * wide(cos_ref)
                   + pltpu.roll(acc, ROPE_DIM // 2, 1) * wide(sina_ref)
                   + pltpu.roll(acc, PROJ_TN - ROPE_DIM // 2, 1) * wide(sinb_ref))
        o_ref[:, cols] = _bf16(acc)


def _layer_block(l, shape, col=0):
    return pl.BlockSpec((None,) + tuple(shape), lambda i: (l, 0, col), pipeline_mode=pl.Buffered(1))


def _qkv_proj(x2, w_in, l, cos_t, sina_t, sinb_t, seq, tm):
    t, d = x2.shape
    n = QKV_COLS
    s_tiles = seq // tm
    tab = pl.BlockSpec((tm, LANES), lambda i: (i % s_tiles, 0))
    rope_lo = 3 * SB_WIDTH // PROJ_TN
    rope_hi = (3 * SB_WIDTH + 2 * MOBA_WIDTH) // PROJ_TN
    return pl.pallas_call(
        functools.partial(_qkv_kernel, rope_lo=rope_lo, rope_hi=rope_hi),
        grid=(t // tm,),
        in_specs=[pl.BlockSpec((tm, d), lambda i: (i, 0)),
                  _layer_block(l, (d, n)), tab, tab, tab],
        out_specs=pl.BlockSpec((tm, n), lambda i: (i, 0)),
        out_shape=jax.ShapeDtypeStruct((t, n), jnp.bfloat16),
        compiler_params=_params(1),
        name="qkv_proj",
    )(x2, w_in, cos_t, sina_t, sinb_t)


def _sb_scores(k_ref, jobs):
    return [_dot_nt(q, k_ref[pl.ds(start, width), tile * LANES:(tile + 1) * LANES])
            for (q, tile, start, width, _, _) in jobs]


def _sb_logs(tri_ref, jobs, zs):
    log_keep, log_beta = [], []
    for (_, _, _, _, _, mask), z in zip(jobs, zs):
        lb = jnp.minimum(z, 0.0) - jnp.log(1.0 + jnp.exp(-jnp.abs(z)))
        lk = lb - z
        log_keep.append(lk if mask is None else jnp.where(mask, lk, 0.0))
        log_beta.append(lb)
    suffix = [_split_dot(lk, tri_ref, job[3]) for job, lk in zip(jobs, log_keep)]
    return log_keep, log_beta, suffix


def _sb_weights(v_ref, jobs, logs):
    log_keep, log_beta, suffix = logs
    out = []
    for u, (_, tile, start, width, (carry, acc), mask) in enumerate(jobs):
        w = jnp.exp(log_beta[u] + suffix[u] + carry)
        w = _bf16(w if mask is None else jnp.where(mask, w, 0.0))
        out.append((carry + suffix[u][:, :1] + log_keep[u][:, :1],
                    acc + _dot(w, v_ref[pl.ds(start, width), tile * LANES:(tile + 1) * LANES])))
    return out


def _attn_kernel(qs_ref, ks_ref, vs_ref, tri_ref, q_ref, k_ref, v_ref, os_ref, o_ref,
                 kmean_ref, knorm_ref, kaug_ref, vt_ref, *, n_blocks):
    i = pl.program_id(2)
    bs = MOBA_BLOCK
    hd = HEAD_DIM
    grp = MOBA_GROUP
    heads = range(ATTN_PAIRS * HEADS_PER_TILE)
    half_of = lambda h: h % HEADS_PER_TILE
    lane = lax.broadcasted_iota(jnp.int32, (bs, LANES), 1)

    tq, win, step = SB_TQ, SB_WINDOW, SB_STEP
    n_tiles = bs // tq
    sb_lane = lax.broadcasted_iota(jnp.int32, (tq, LANES), 1)
    sb_row = lax.broadcasted_iota(jnp.int32, (tq, win), 0)
    sb_col = lax.broadcasted_iota(jnp.int32, (tq, win), 1)
    zero = (jnp.zeros((tq, 1), jnp.float32), jnp.zeros((tq, LANES), jnp.float32))
    sb_jobs, sb_starts, sb_q = [], [], []
    for t in range(n_tiles):
        first_q = i * bs + t * tq
        start0 = pl.multiple_of(jnp.maximum(first_q + tq - win, 0), step)
        past = sb_col - sb_row < first_q - start0
        sb_starts.append(start0)
        for pr in range(ATTN_PAIRS):
            q = qs_ref[t * tq:(t + 1) * tq, pr * LANES:(pr + 1) * LANES]
            for h in range(HEADS_PER_TILE):
                qh = jnp.where(sb_lane // hd == h, q, jnp.zeros_like(q)) * SCALE
                sb_q.append(qh)
                sb_jobs.append((qh, pr, start0, win, zero, past))
    per_tile = ATTN_PAIRS * HEADS_PER_TILE

    @pl.when(i == 0)
    def _():
        kmean_ref[...] = jnp.zeros_like(kmean_ref)
        r_i = lax.broadcasted_iota(jnp.int32, (8 * HEADS_PER_TILE, LANES), 0)
        l_i = lax.broadcasted_iota(jnp.int32, (8 * HEADS_PER_TILE, LANES), 1)
        head_lanes = _bf16(r_i // 8 == l_i // hd)
        norm2 = [jnp.zeros((8 * HEADS_PER_TILE, bs), jnp.float32)] * ATTN_PAIRS
        for n in range(n_blocks):
            kbs = [k_ref[n * bs:(n + 1) * bs, pr * LANES:(pr + 1) * LANES].astype(jnp.float32)
                   for pr in range(ATTN_PAIRS)]
            for pr in range(ATTN_PAIRS):
                kmean_ref[pr, hd + n:hd + n + 1, :] = jnp.sum(kbs[pr], axis=0, keepdims=True) * (1.0 / bs)
                norm2[pr] = jnp.maximum(norm2[pr], _dot_nt(head_lanes, _bf16(kbs[pr] * kbs[pr])))
            tag = (lane == hd + n).astype(jnp.float32)
            for h in heads:
                kb = kbs[h // HEADS_PER_TILE]
                dims = kb if half_of(h) == 0 else pltpu.roll(kb, LANES - half_of(h) * hd, 1)
                kaug_ref[h, n * bs:(n + 1) * bs, :] = _bf16(jnp.where(lane < hd, dims, tag))
            vt_ref[n] = _bf16(v_ref[n * bs:(n + 1) * bs, :].astype(jnp.float32).T)
        for h in heads:
            per_key = norm2[h // HEADS_PER_TILE][8 * half_of(h):8 * half_of(h) + 1]
            knorm_ref[h] = jnp.broadcast_to(jnp.sqrt(jnp.max(per_key, axis=1, keepdims=True)), (1, LANES))

    qs_in = [q_ref[:, pr * LANES:(pr + 1) * LANES] for pr in range(ATTN_PAIRS)]
    qfs = [qq.astype(jnp.float32) for qq in qs_in]
    key = lax.broadcasted_iota(jnp.int32, (bs, bs), 0)
    qry = lax.broadcasted_iota(jnp.int32, (bs, bs), 1)
    rows = lax.broadcasted_iota(jnp.int32, (MOBA_TAGS, bs), 0)
    causal = key <= qry
    kmeans = [kmean_ref[pr, hd:hd + MOBA_TAGS, :] for pr in range(ATTN_PAIRS)]
    km_his = [_bf16(km) for km in kmeans]
    km_los = [_bf16(km - hi.astype(jnp.float32)) for km, hi in zip(kmeans, km_his)]
    own_start = pl.multiple_of(i * bs, bs)
    vt_own = vt_ref[i]

    dims, sel, shift = [], [], []
    for h in heads:
        q, qf = qs_in[h // HEADS_PER_TILE], qfs[h // HEADS_PER_TILE]
        km_hi, km_lo = km_his[h // HEADS_PER_TILE], km_los[h // HEADS_PER_TILE]
        qh = jnp.where(lane // hd == half_of(h), q, jnp.zeros_like(q))
        gate = _dot_nt(km_hi, qh) + _dot_nt(km_lo, qh)
        gate = jnp.where(rows < i, gate, -jnp.inf)
        chosen = jnp.zeros((MOBA_TAGS, bs), jnp.float32)
        for _ in range(MOBA_TOPK):
            best = jnp.max(gate, axis=0, keepdims=True)
            first = jnp.min(jnp.where(gate == best, rows, MOBA_TAGS), axis=0, keepdims=True)
            pick = (rows == first) & (best > -jnp.inf)
            chosen = jnp.where(pick, 1.0, chosen)
            gate = jnp.where(pick, -jnp.inf, gate)
        pad = jnp.zeros((hd, bs), jnp.float32)
        by_lane = jnp.concatenate([pad, chosen, jnp.zeros((LANES - hd - MOBA_TAGS, bs), jnp.float32)], axis=0).T
        sel.append(by_lane > 0.0)
        d = (qf if half_of(h) == 0 else pltpu.roll(qf, LANES - half_of(h) * hd, 1)) * SCALE
        dims.append(d)
        norm = jnp.sqrt(jnp.sum(jnp.where(lane < hd, d * d, 0.0), axis=1, keepdims=True))
        shift.append(norm * knorm_ref[h] * MOBA_NORM_SLACK)

    def finish(state):
        out_t = jnp.concatenate([acc / l for (l, acc) in state], axis=0)
        return out_t.T

    def walk(q_past, step_fn, state, pair_groups):
        def scores(blk):
            start = pl.multiple_of(blk * bs, bs)
            return tuple(_dot_nt(kaug_ref[h, pl.ds(start, bs), :], q_past[h]) for h in heads)

        def visit(first_blk, n_blk, state):
            state = list(state)
            s_next = scores(first_blk)
            for c in range(n_blk):
                s_cur, s_next = s_next, (scores(first_blk + c + 1) if c + 1 < n_blk else None)
                for h in heads:
                    state[h] = step_fn(state[h], s_cur[h], vt_ref[first_blk + c, h * hd:(h + 1) * hd, :])
            return tuple(state)

        n_groups = (i + grp - 1) // grp
        done = 0
        if pair_groups:
            state = lax.fori_loop(0, n_groups // 2, lambda g, st: visit(g * 2 * grp, 2 * grp, st), tuple(state))
            done = n_groups // 2 * 2
        return lax.fori_loop(done, n_groups, lambda g, st: visit(g * grp, grp, st), tuple(state))

    def shifted():
        sb_z = _sb_scores(ks_ref, sb_jobs)
        q_past = [_bf16(jnp.where(lane < hd, dims[h], jnp.where(sel[h], -shift[h], NEG_BIG))) for h in heads]
        own = [_dot_nt(kaug_ref[h, pl.ds(own_start, bs), :],
                       _bf16(jnp.where(lane < hd, dims[h], jnp.where(lane == hd + i, -shift[h], 0.0))))
               for h in heads]
        sb_logs = _sb_logs(tri_ref, sb_jobs, sb_z)
        state = []
        for h in heads:
            p = jnp.where(causal, jnp.exp(own[h]), 0.0)
            state.append((jnp.sum(p, axis=0, keepdims=True), _dot(vt_own[h * hd:(h + 1) * hd, :], _bf16(p))))
        sb_state = _sb_weights(vs_ref, sb_jobs, sb_logs)

        def step_fn(st, s, vt):
            p = jnp.exp(s)
            return st[0] + jnp.sum(p, axis=0, keepdims=True), st[1] + _dot(vt, _bf16(p))

        return finish(walk(q_past, step_fn, state, True)), tuple(sb_state)

    def running_max():
        sb_state = _sb_weights(vs_ref, sb_jobs, _sb_logs(tri_ref, sb_jobs, _sb_scores(ks_ref, sb_jobs)))
        q_past, state = [], []
        for h in heads:
            q_past.append(_bf16(jnp.where(lane < hd, dims[h], jnp.where(sel[h], 0.0, NEG_BIG))))
            q_own = _bf16(jnp.where(lane < hd, dims[h], 0.0))
            s = jnp.where(causal, _dot_nt(kaug_ref[h, pl.ds(own_start, bs), :], q_own), NEG_BIG)
            m = jnp.max(s, axis=0, keepdims=True)
            p = jnp.exp(s - m)
            state.append((m, jnp.sum(p, axis=0, keepdims=True), _dot(vt_own[h * hd:(h + 1) * hd, :], _bf16(p))))

        def step_fn(st, s, vt):
            m_old, l_old, acc_old = st
            m_new = jnp.maximum(m_old, jnp.max(s, axis=0, keepdims=True))
            p = jnp.exp(s - m_new)
            alpha = jnp.exp(m_old - m_new)
            return m_new, alpha * l_old + jnp.sum(p, axis=0, keepdims=True), alpha * acc_old + _dot(vt, _bf16(p))

        return finish([st[1:] for st in walk(q_past, step_fn, state, False)]), tuple(sb_state)

    largest = jnp.max(functools.reduce(jnp.maximum, shift))
    moba_out, sb_state = lax.cond(2.0 * largest <= MOBA_SHIFT_MAX, shifted, running_max)
    o_ref[...] = _bf16(moba_out)

    def worst_bound(state, k):
        parts = []
        for t in range(n_tiles):
            both = functools.reduce(jnp.maximum, [state[t * per_tile + u][0] for u in range(per_tile)])
            parts.append(jnp.where(sb_starts[t] - k * step > 0, both, NEG_BIG))
        return jnp.max(functools.reduce(jnp.maximum, parts))

    def cond(loop):
        return loop[1] > SB_UNDERFLOW

    def body(loop):
        k, _, state = loop
        jobs = []
        for t in range(n_tiles):
            left = sb_starts[t] - k * step
            start = pl.multiple_of(jnp.maximum(left - step, 0), step)
            live = jnp.broadcast_to(left > 0, (tq, step))
            for u in range(per_tile):
                job = t * per_tile + u
                jobs.append((sb_q[job], sb_jobs[job][1], start, step, state[job], live))
        state = tuple(_sb_weights(vs_ref, jobs, _sb_logs(tri_ref, jobs, _sb_scores(ks_ref, jobs))))
        return k + 1, worst_bound(state, k + 1), state

    _, _, sb_state = lax.while_loop(cond, body, (0, worst_bound(sb_state, 0), sb_state))
    for t in range(n_tiles):
        for pr in range(ATTN_PAIRS):
            job = t * per_tile + pr * HEADS_PER_TILE
            os_ref[t * tq:(t + 1) * tq, pr * LANES:(pr + 1) * LANES] = _bf16(
                jnp.where(sb_lane < hd, sb_state[job][1], sb_state[job + 1][1]))


def _attention(qkv, batch, seq):
    t = qkv.shape[0]
    nq = seq // MOBA_BLOCK
    assert nq % MOBA_GROUP == 0, "the grouped block walk reads whole groups"
    assert nq <= MOBA_TAGS, "one block tag lane per key block"
    assert MOBA_BLOCK % SB_TQ == 0 and seq >= SB_WINDOW and SB_WIDTH == MOBA_WIDTH
    width = ATTN_PAIRS * LANES
    hp = MOBA_WIDTH // width
    n_heads = ATTN_PAIRS * HEADS_PER_TILE
    idx = jnp.arange(SB_WINDOW)
    tri = _bf16(idx[:, None] > idx[None, :])
    tri2 = jnp.concatenate([tri, tri], axis=0)
    rows_blk = lambda sec: pl.BlockSpec((MOBA_BLOCK, width), lambda b, p, i: (b * nq + i, sec * hp + p))
    seq_blk = lambda sec, bufs: pl.BlockSpec((seq, width), lambda b, p, i: (b, sec * hp + p),
                                             pipeline_mode=pl.Buffered(bufs))
    out_blk = pl.BlockSpec((MOBA_BLOCK, width), lambda b, p, i: (b * nq + i, p))
    return pl.pallas_call(
        functools.partial(_attn_kernel, n_blocks=nq),
        grid=(batch, hp, nq),
        in_specs=[rows_blk(0), seq_blk(1, 2), seq_blk(2, 2),
                  pl.BlockSpec((2 * SB_WINDOW, SB_WINDOW), lambda b, p, i: (0, 0)),
                  rows_blk(3), seq_blk(4, 1), seq_blk(5, 1)],
        out_specs=[out_blk, out_blk],
        out_shape=[jax.ShapeDtypeStruct((t, SB_WIDTH), jnp.bfloat16),
                   jax.ShapeDtypeStruct((t, MOBA_WIDTH), jnp.bfloat16)],
        scratch_shapes=[pltpu.VMEM((ATTN_PAIRS, LANES, LANES), jnp.float32),
                        pltpu.VMEM((n_heads, 1, LANES), jnp.float32),
                        pltpu.VMEM((n_heads, seq, LANES), jnp.bfloat16),
                        pltpu.VMEM((nq, width, MOBA_BLOCK), jnp.bfloat16)],
        compiler_params=_params(3),
        name="attention",
    )(qkv, qkv, qkv, tri2, qkv, qkv, qkv)


def _mix_kernel(x_ref, osb_ref, omb_ref, wgs_ref, wgm_ref, wbs_ref, wbm_ref, wo_ref, g_ref, b_ref, o_ref, *, alpha):
    x = x_ref[...]
    xb = _bf16(x)
    branch_sb = _dot(osb_ref[...], wbs_ref[...])
    branch_mb = _dot(omb_ref[...], wbm_ref[...])
    merged = (jax.nn.sigmoid(_dot(xb, wgs_ref[...])) * branch_sb
              + jax.nn.sigmoid(_dot(xb, wgm_ref[...])) * branch_mb)
    mix = _dot(_bf16(merged), wo_ref[...])
    o_ref[...] = _layer_norm(alpha * x + mix, g_ref[...], b_ref[...])


def _mix_block(x2, o_sb, o_mb, w_in, w_bsb, w_bmb, w_out, ln_g, ln_b, l, alpha, tm):
    t, d = x2.shape
    assert QKV_COLS % d == 0
    rows = lambda w: pl.BlockSpec((tm, w), lambda i: (i, 0))
    whole = lambda a: _layer_block(l, a.shape[1:])
    gate_cols = lambda j: _layer_block(l, (d, d), QKV_COLS // d + j)
    return pl.pallas_call(
        functools.partial(_mix_kernel, alpha=alpha),
        grid=(t // tm,),
        in_specs=[rows(d), rows(SB_WIDTH), rows(MOBA_WIDTH),
                  gate_cols(0), gate_cols(1), whole(w_bsb), whole(w_bmb), whole(w_out), whole(ln_g), whole(ln_b)],
        out_specs=rows(d),
        out_shape=jax.ShapeDtypeStruct((t, d), jnp.float32),
        compiler_params=_params(1),
        name="mix_ln",
    )(x2, o_sb, o_mb, w_in, w_in, w_bsb, w_bmb, w_out, ln_g, ln_b)


def _ffn_kernel(x_ref, wg_ref, wu_ref, wd_ref, g_ref, b_ref, o_ref, *, alpha, chunk):
    x = x_ref[...]
    xb = _bf16(x)
    acc = alpha * x
    for c in range(0, wg_ref.shape[1], chunk):
        gate = _dot(xb, wg_ref[:, c:c + chunk])
        up = _dot(xb, wu_ref[:, c:c + chunk])
        acc = acc + _dot(_bf16(jax.nn.silu(gate) * up), wd_ref[c:c + chunk, :])
    o_ref[...] = _layer_norm(acc, g_ref[...], b_ref[...])


def _ffn_chunk(d_ff):
    best = LANES
    for c in range(LANES, 1536 + 1, LANES):
        if d_ff % c == 0:
            best = c
    return best


def _ffn_block(x2, w_gate, w_up, w_down, ln_g, ln_b, l, alpha, tm):
    t, d = x2.shape
    rows = pl.BlockSpec((tm, d), lambda i: (i, 0))
    whole = lambda a: _layer_block(l, a.shape[1:])
    return pl.pallas_call(
        functools.partial(_ffn_kernel, alpha=alpha, chunk=_ffn_chunk(w_gate.shape[2])),
        grid=(t // tm,),
        in_specs=[rows, whole(w_gate), whole(w_up), whole(w_down), whole(ln_g), whole(ln_b)],
        out_specs=rows,
        out_shape=jax.ShapeDtypeStruct((t, d), jnp.float32),
        compiler_params=_params(1),
        name="ffn_ln",
    )(x2, w_gate, w_up, w_down, ln_g, ln_b)


def _rope_tables(seq):
    half = ROPE_DIM // 2
    inv_freq = ROPE_THETA ** (-jnp.arange(0, ROPE_DIM, 2, dtype=jnp.float32) / ROPE_DIM)
    dim = np.arange(LANES) % HEAD_DIM
    ang = jnp.arange(seq).astype(jnp.float32)[:, None] * inv_freq[dim % half][None, :]
    cos, sin = jnp.cos(ang), jnp.sin(ang)
    rotary = dim < ROPE_DIM
    return (jnp.where(rotary, cos, 1.0),
            jnp.where(rotary & (dim >= half), sin, 0.0),
            jnp.where(dim < half, -sin, 0.0))


def kernel(x, w_in, w_branch_sb, w_branch_moba, w_out, ln_mix_g, ln_mix_b,
           w_ffn_gate, w_ffn_up, w_ffn_down, ln_ffn_g, ln_ffn_b):
    batch, seq, d = x.shape
    depth = w_in.shape[0]
    assert w_in.shape[2] == QKV_COLS + 2 * d
    assert seq % MOBA_BLOCK == 0
    alpha = (2 * depth) ** 0.25
    tm = min(512, seq)
    cos_t, sina_t, sinb_t = _rope_tables(seq)
    x2 = x.reshape(batch * seq, d)
    w_in, w_bsb, w_bmb, w_o = _bf16(w_in), _bf16(w_branch_sb), _bf16(w_branch_moba), _bf16(w_out)
    w_fg, w_fu, w_fd = _bf16(w_ffn_gate), _bf16(w_ffn_up), _bf16(w_ffn_down)
    row = lambda a: a[:, None, :]
    for l in range(depth):
        qkv = _qkv_proj(x2, w_in, l, cos_t, sina_t, sinb_t, seq, tm)
        o_sb, o_mb = _attention(qkv, batch, seq)
        x2 = _mix_block(x2, o_sb, o_mb, w_in, w_bsb, w_bmb, w_o, row(ln_mix_g), row(ln_mix_b), l, alpha, tm)
        x2 = _ffn_block(x2, w_fg, w_fu, w_fd, row(ln_ffn_g), row(ln_ffn_b), l, alpha, tm)
    return x2.reshape(batch, seq, d)
```

```python
import functools

import jax
import jax.numpy as jnp
import numpy as np
from jax import lax
from jax.experimental import pallas as pl
from jax.experimental.pallas import tpu as pltpu

HEAD_DIM = 64
SB_HEADS = 8
MOBA_HEADS = 8
SB_WIDTH = SB_HEADS * HEAD_DIM
MOBA_WIDTH = MOBA_HEADS * HEAD_DIM
QKV_COLS = 3 * SB_WIDTH + 3 * MOBA_WIDTH
MOBA_BLOCK = 256
MOBA_TOPK = 3
ROPE_THETA = 500000.0
ROPE_DIM = HEAD_DIM // 4
LN_EPS = 1e-5
SCALE = HEAD_DIM ** -0.5

LANES = 128
HEADS_PER_TILE = LANES // HEAD_DIM
VMEM_LIMIT = 56 * 1024 * 1024
MXU_WIDTH = 256
MIX_CHUNKS = 2
PROJ_TN = 512
SB_TQ = 64
SB_WINDOW = 256
SB_STEP = 64
MOBA_GROUP = 4
ATTN_PAIRS = 2
MOBA_TAGS = 32
MOBA_SHIFT_MAX = 60.0
MOBA_NORM_SLACK = 1.01
NEG_BIG = -1e30
SB_UNDERFLOW = -104.0

_NT = (((1,), (1,)), ((), ()))


def _bf16(a):
    return a.astype(jnp.bfloat16)


def _dot(a, b):
    return jnp.dot(a, b, preferred_element_type=jnp.float32)


def _dot_nt(a, b):
    return lax.dot_general(a, b, _NT, preferred_element_type=jnp.float32)


def _split_dot(a_f32, tri2_ref, width):
    hi = _bf16(a_f32)
    lo = _bf16(a_f32 - hi.astype(jnp.float32))
    if width == SB_WINDOW:
        return _dot(jnp.concatenate([hi, lo], axis=1), tri2_ref[...])
    tri = tri2_ref[:width, :width]
    return _dot(jnp.concatenate([hi, lo], axis=1), jnp.concatenate([tri, tri], axis=0))


def _layer_norm(y, g, b):
    mu = jnp.mean(y, axis=-1, keepdims=True)
    d = y - mu
    var = jnp.mean(d * d, axis=-1, keepdims=True)
    return d * lax.rsqrt(var + LN_EPS) * g + b


def _params(n_grid):
    return pltpu.CompilerParams(dimension_semantics=("arbitrary",) * n_grid,
                                vmem_limit_bytes=VMEM_LIMIT)


def _qkv_kernel(x_ref, w_ref, cos_ref, sina_ref, sinb_ref, o_ref, *, rope_lo, rope_hi):
    xb = _bf16(x_ref[...])
    reps = PROJ_TN // LANES
    wide = lambda r: jnp.concatenate([r[...]] * reps, axis=1)
    for j in range(w_ref.shape[1] // PROJ_TN):
        cols = slice(j * PROJ_TN, (j + 1) * PROJ_TN)
        acc = _dot(xb, w_ref[:, cols])
        if rope_lo <= j < rope_hi:
            acc = (acc * wide(cos_ref)
                   + pltpu.roll(acc, ROPE_DIM // 2, 1) * wide(sina_ref)
                   + pltpu.roll(acc, PROJ_TN - ROPE_DIM // 2, 1) * wide(sinb_ref))
        o_ref[:, cols] = _bf16(acc)


def _layer_block(l, shape, col=0):
    return pl.BlockSpec((None,) + tuple(shape), lambda i: (l, 0, col), pipeline_mode=pl.Buffered(1))


def _qkv_proj(x2, w_in, l, cos_t, sina_t, sinb_t, seq, tm):
    t, d = x2.shape
    n = QKV_COLS
    s_tiles = seq // tm
    tab = pl.BlockSpec((tm, LANES), lambda i: (i % s_tiles, 0))
    rope_lo = 3 * SB_WIDTH // PROJ_TN
    rope_hi = (3 * SB_WIDTH + 2 * MOBA_WIDTH) // PROJ_TN
    return pl.pallas_call(
        functools.partial(_qkv_kernel, rope_lo=rope_lo, rope_hi=rope_hi),
        grid=(t // tm,),
        in_specs=[pl.BlockSpec((tm, d), lambda i: (i, 0)),
                  _layer_block(l, (d, n)), tab, tab, tab],
        out_specs=pl.BlockSpec((tm, n), lambda i: (i, 0)),
        out_shape=jax.ShapeDtypeStruct((t, n), jnp.bfloat16),
        compiler_params=_params(1),
        name="qkv_proj",
    )(x2, w_in, cos_t, sina_t, sinb_t)


def _sb_scores(k_ref, jobs):
    return [_dot_nt(q, k_ref[pl.ds(start, width), tile * LANES:(tile + 1) * LANES])
            for (q, tile, start, width, _, _) in jobs]


def _sb_logs(tri_ref, jobs, zs):
    log_keep, log_beta = [], []
    for (_, _, _, _, _, mask), z in zip(jobs, zs):
        lb = jnp.minimum(z, 0.0) - jnp.log(1.0 + jnp.exp(-jnp.abs(z)))
        lk = lb - z
        log_keep.append(lk if mask is None else jnp.where(mask, lk, 0.0))
        log_beta.append(lb)
    suffix = [_split_dot(lk, tri_ref, job[3]) for job, lk in zip(jobs, log_keep)]
    return log_keep, log_beta, suffix


def _sb_weights(v_ref, jobs, logs):
    log_keep, log_beta, suffix = logs
    out = []
    for u, (_, tile, start, width, (carry, acc), mask) in enumerate(jobs):
        w = jnp.exp(log_beta[u] + suffix[u] + carry)
        w = _bf16(w if mask is None else jnp.where(mask, w, 0.0))
        out.append((carry + suffix[u][:, :1] + log_keep[u][:, :1],
                    acc + _dot(w, v_ref[pl.ds(start, width), tile * LANES:(tile + 1) * LANES])))
    return out


def _attn_kernel(qs_ref, ks_ref, vs_ref, tri_ref, q_ref, k_ref, v_ref, os_ref, o_ref,
                 kmean_ref, knorm_ref, kaug_ref, vt_ref, *, n_blocks):
    i = pl.program_id(2)
    bs = MOBA_BLOCK
    hd = HEAD_DIM
    grp = MOBA_GROUP
    heads = range(ATTN_PAIRS * HEADS_PER_TILE)
    half_of = lambda h: h % HEADS_PER_TILE
    lane = lax.broadcasted_iota(jnp.int32, (bs, LANES), 1)

    tq, win, step = SB_TQ, SB_WINDOW, SB_STEP
    n_tiles = bs // tq
    sb_lane = lax.broadcasted_iota(jnp.int32, (tq, LANES), 1)
    sb_row = lax.broadcasted_iota(jnp.int32, (tq, win), 0)
    sb_col = lax.broadcasted_iota(jnp.int32, (tq, win), 1)
    zero = (jnp.zeros((tq, 1), jnp.float32), jnp.zeros((tq, LANES), jnp.float32))
    sb_jobs, sb_starts, sb_q = [], [], []
    for t in range(n_tiles):
        first_q = i * bs + t * tq
        start0 = pl.multiple_of(jnp.maximum(first_q + tq - win, 0), step)
        past = sb_col - sb_row < first_q - start0
        sb_starts.append(start0)
        for pr in range(ATTN_PAIRS):
            q = qs_ref[t * tq:(t + 1) * tq, pr * LANES:(pr + 1) * LANES]
            for h in range(HEADS_PER_TILE):
                qh = jnp.where(sb_lane // hd == h, q, jnp.zeros_like(q)) * SCALE
                sb_q.append(qh)
                sb_jobs.append((qh, pr, start0, win, zero, past))
    per_tile = ATTN_PAIRS * HEADS_PER_TILE

    @pl.when(i == 0)
    def _():
        kmean_ref[...] = jnp.zeros_like(kmean_ref)
        r_i = lax.broadcasted_iota(jnp.int32, (8 * HEADS_PER_TILE, LANES), 0)
        l_i = lax.broadcasted_iota(jnp.int32, (8 * HEADS_PER_TILE, LANES), 1)
        head_lanes = _bf16(r_i // 8 == l_i // hd)
        norm2 = [jnp.zeros((8 * HEADS_PER_TILE, bs), jnp.float32)] * ATTN_PAIRS
        for n in range(n_blocks):
            kbs = [k_ref[n * bs:(n + 1) * bs, pr * LANES:(pr + 1) * LANES].astype(jnp.float32)
                   for pr in range(ATTN_PAIRS)]
            for pr in range(ATTN_PAIRS):
                kmean_ref[pr, hd + n:hd + n + 1, :] = jnp.sum(kbs[pr], axis=0, keepdims=True) * (1.0 / bs)
                norm2[pr] = jnp.maximum(norm2[pr], _dot_nt(head_lanes, _bf16(kbs[pr] * kbs[pr])))
            tag = (lane == hd + n).astype(jnp.float32)
            for h in heads:
                kb = kbs[h // HEADS_PER_TILE]
                dims = kb if half_of(h) == 0 else pltpu.roll(kb, LANES - half_of(h) * hd, 1)
                kaug_ref[h, n * bs:(n + 1) * bs, :] = _bf16(jnp.where(lane < hd, dims, tag))
            vt_ref[n] = _bf16(v_ref[n * bs:(n + 1) * bs, :].astype(jnp.float32).T)
        for h in heads:
            per_key = norm2[h // HEADS_PER_TILE][8 * half_of(h):8 * half_of(h) + 1]
            knorm_ref[h] = jnp.broadcast_to(jnp.sqrt(jnp.max(per_key, axis=1, keepdims=True)), (1, LANES))

    qs_in = [q_ref[:, pr * LANES:(pr + 1) * LANES] for pr in range(ATTN_PAIRS)]
    qfs = [qq.astype(jnp.float32) for qq in qs_in]
    key = lax.broadcasted_iota(jnp.int32, (bs, bs), 0)
    qry = lax.broadcasted_iota(jnp.int32, (bs, bs), 1)
    rows = lax.broadcasted_iota(jnp.int32, (MOBA_TAGS, bs), 0)
    causal = key <= qry
    kmeans = [kmean_ref[pr, hd:hd + MOBA_TAGS, :] for pr in range(ATTN_PAIRS)]
    km_his = [_bf16(km) for km in kmeans]
    km_los = [_bf16(km - hi.astype(jnp.float32)) for km, hi in zip(kmeans, km_his)]
    own_start = pl.multiple_of(i * bs, bs)
    vt_own = vt_ref[i]

    dims, shift = [], []
    for h in heads:
        qf = qfs[h // HEADS_PER_TILE]
        d = (qf if half_of(h) == 0 else pltpu.roll(qf, LANES - half_of(h) * hd, 1)) * SCALE
        dims.append(d)
        norm = jnp.sqrt(jnp.sum(jnp.where(lane < hd, d * d, 0.0), axis=1, keepdims=True))
        shift.append(norm * knorm_ref[h] * MOBA_NORM_SLACK)

    def select():
        sel = []
        for h in heads:
            q = qs_in[h // HEADS_PER_TILE]
            km_hi, km_lo = km_his[h // HEADS_PER_TILE], km_los[h // HEADS_PER_TILE]
            qh = jnp.where(lane // hd == half_of(h), q, jnp.zeros_like(q))
            gate = _dot_nt(km_hi, qh) + _dot_nt(km_lo, qh)
            gate = jnp.where(rows < i, gate, -jnp.inf)
            chosen = jnp.zeros((MOBA_TAGS, bs), jnp.float32)
            for _ in range(MOBA_TOPK):
                best = jnp.max(gate, axis=0, keepdims=True)
                first = jnp.min(jnp.where(gate == best, rows, MOBA_TAGS), axis=0, keepdims=True)
                pick = (rows == first) & (best > -jnp.inf)
                chosen = jnp.where(pick, 1.0, chosen)
                gate = jnp.where(pick, -jnp.inf, gate)
            pad = jnp.zeros((hd, bs), jnp.float32)
            by_lane = jnp.concatenate([pad, chosen, jnp.zeros((LANES - hd - MOBA_TAGS, bs), jnp.float32)], axis=0).T
            sel.append(by_lane > 0.0)
        return sel

    def finish(state):
        out_t = jnp.concatenate([acc / l for (l, acc) in state], axis=0)
        return out_t.T

    def walk(q_past, step_fn, state, pair_groups):
        def scores(blk):
            start = pl.multiple_of(blk * bs, bs)
            return tuple(_dot_nt(kaug_ref[h, pl.ds(start, bs), :], q_past[h]) for h in heads)

        def visit(first_blk, n_blk, state):
            state = list(state)
            s_next = scores(first_blk)
            for c in range(n_blk):
                s_cur, s_next = s_next, (scores(first_blk + c + 1) if c + 1 < n_blk else None)
                for h in heads:
                    state[h] = step_fn(state[h], s_cur[h], vt_ref[first_blk + c, h * hd:(h + 1) * hd, :])
            return tuple(state)

        n_groups = (i + grp - 1) // grp
        done = 0
        if pair_groups:
            state = lax.fori_loop(0, n_groups // 2, lambda g, st: visit(g * 2 * grp, 2 * grp, st), tuple(state))
            done = n_groups // 2 * 2
        return lax.fori_loop(done, n_groups, lambda g, st: visit(g * grp, grp, st), tuple(state))

    def shifted():
        sb_z = _sb_scores(ks_ref, sb_jobs)
        sel = select()
        q_past = [_bf16(jnp.where(lane < hd, dims[h], jnp.where(sel[h], -shift[h], NEG_BIG))) for h in heads]
        own = [_dot_nt(kaug_ref[h, pl.ds(own_start, bs), :],
                       _bf16(jnp.where(lane < hd, dims[h], jnp.where(lane == hd + i, -shift[h], 0.0))))
               for h in heads]
        sb_logs = _sb_logs(tri_ref, sb_jobs, sb_z)
        state = []
        for h in heads:
            p = jnp.where(causal, jnp.exp(own[h]), 0.0)
            state.append((jnp.sum(p, axis=0, keepdims=True), _dot(vt_own[h * hd:(h + 1) * hd, :], _bf16(p))))
        sb_state = _sb_weights(vs_ref, sb_jobs, sb_logs)

        def step_fn(st, s, vt):
            p = jnp.exp(s)
            return st[0] + jnp.sum(p, axis=0, keepdims=True), st[1] + _dot(vt, _bf16(p))

        return finish(walk(q_past, step_fn, state, True)), tuple(sb_state)

    def running_max():
        sb_state = _sb_weights(vs_ref, sb_jobs, _sb_logs(tri_ref, sb_jobs, _sb_scores(ks_ref, sb_jobs)))
        sel = select()
        q_past, state = [], []
        for h in heads:
            q_past.append(_bf16(jnp.where(lane < hd, dims[h], jnp.where(sel[h], 0.0, NEG_BIG))))
            q_own = _bf16(jnp.where(lane < hd, dims[h], 0.0))
            s = jnp.where(causal, _dot_nt(kaug_ref[h, pl.ds(own_start, bs), :], q_own), NEG_BIG)
            m = jnp.max(s, axis=0, keepdims=True)
            p = jnp.exp(s - m)
            state.append((m, jnp.sum(p, axis=0, keepdims=True), _dot(vt_own[h * hd:(h + 1) * hd, :], _bf16(p))))

        def step_fn(st, s, vt):
            m_old, l_old, acc_old = st
            m_new = jnp.maximum(m_old, jnp.max(s, axis=0, keepdims=True))
            p = jnp.exp(s - m_new)
            alpha = jnp.exp(m_old - m_new)
            return m_new, alpha * l_old + jnp.sum(p, axis=0, keepdims=True), alpha * acc_old + _dot(vt, _bf16(p))

        return finish([st[1:] for st in walk(q_past, step_fn, state, False)]), tuple(sb_state)

    largest = jnp.max(functools.reduce(jnp.maximum, shift))
    moba_out, sb_state = lax.cond(2.0 * largest <= MOBA_SHIFT_MAX, shifted, running_max)
    o_ref[...] = _bf16(moba_out)

    def worst_bound(state, k):
        parts = []
        for t in range(n_tiles):
            both = functools.reduce(jnp.maximum, [state[t * per_tile + u][0] for u in range(per_tile)])
            parts.append(jnp.where(sb_starts[t] - k * step > 0, both, NEG_BIG))
        return jnp.max(functools.reduce(jnp.maximum, parts))

    def cond(loop):
        return loop[1] > SB_UNDERFLOW

    def body(loop):
        k, _, state = loop
        jobs = []
        for t in range(n_tiles):
            left = sb_starts[t] - k * step
            start = pl.multiple_of(jnp.maximum(left - step, 0), step)
            live = jnp.broadcast_to(left > 0, (tq, step))
            for u in range(per_tile):
                job = t * per_tile + u
                jobs.append((sb_q[job], sb_jobs[job][1], start, step, state[job], live))
        state = tuple(_sb_weights(vs_ref, jobs, _sb_logs(tri_ref, jobs, _sb_scores(ks_ref, jobs))))
        return k + 1, worst_bound(state, k + 1), state

    _, _, sb_state = lax.while_loop(cond, body, (0, worst_bound(sb_state, 0), sb_state))
    for t in range(n_tiles):
        for pr in range(ATTN_PAIRS):
            job = t * per_tile + pr * HEADS_PER_TILE
            os_ref[t * tq:(t + 1) * tq, pr * LANES:(pr + 1) * LANES] = _bf16(
                jnp.where(sb_lane < hd, sb_state[job][1], sb_state[job + 1][1]))


def _attention(qkv, batch, seq):
    t = qkv.shape[0]
    nq = seq // MOBA_BLOCK
    assert nq % MOBA_GROUP == 0, "the grouped block walk reads whole groups"
    assert nq <= MOBA_TAGS, "one block tag lane per key block"
    assert MOBA_BLOCK % SB_TQ == 0 and seq >= SB_WINDOW and SB_WIDTH == MOBA_WIDTH
    width = ATTN_PAIRS * LANES
    hp = MOBA_WIDTH // width
    n_heads = ATTN_PAIRS * HEADS_PER_TILE
    idx = jnp.arange(SB_WINDOW)
    tri = _bf16(idx[:, None] > idx[None, :])
    tri2 = jnp.concatenate([tri, tri], axis=0)
    rows_blk = lambda sec: pl.BlockSpec((MOBA_BLOCK, width), lambda b, p, i: (b * nq + i, sec * hp + p))
    seq_blk = lambda sec, bufs: pl.BlockSpec((seq, width), lambda b, p, i: (b, sec * hp + p),
                                             pipeline_mode=pl.Buffered(bufs))
    out_blk = pl.BlockSpec((MOBA_BLOCK, width), lambda b, p, i: (b * nq + i, p))
    return pl.pallas_call(
        functools.partial(_attn_kernel, n_blocks=nq),
        grid=(batch, hp, nq),
        in_specs=[rows_blk(0), seq_blk(1, 2), seq_blk(2, 2),
                  pl.BlockSpec((2 * SB_WINDOW, SB_WINDOW), lambda b, p, i: (0, 0)),
                  rows_blk(3), seq_blk(4, 1), seq_blk(5, 1)],
        out_specs=[out_blk, out_blk],
        out_shape=[jax.ShapeDtypeStruct((t, SB_WIDTH), jnp.bfloat16),
                   jax.ShapeDtypeStruct((t, MOBA_WIDTH), jnp.bfloat16)],
        scratch_shapes=[pltpu.VMEM((ATTN_PAIRS, LANES, LANES), jnp.float32),
                        pltpu.VMEM((n_heads, 1, LANES), jnp.float32),
                        pltpu.VMEM((n_heads, seq, LANES), jnp.bfloat16),
                        pltpu.VMEM((nq, width, MOBA_BLOCK), jnp.bfloat16)],
        compiler_params=_params(3),
        name="attention",
    )(qkv, qkv, qkv, tri2, qkv, qkv, qkv)


def _mix_kernel(x_ref, osb_ref, omb_ref, wgs_ref, wgm_ref, wbs_ref, wbm_ref, wo_ref, g_ref, b_ref, o_ref, *, alpha):
    rows = x_ref.shape[0] // MIX_CHUNKS
    for c in range(MIX_CHUNKS):
        r = slice(c * rows, (c + 1) * rows)
        x = x_ref[r, :]
        xb = _bf16(x)
        branch_sb = _dot(osb_ref[r, :], wbs_ref[...])
        branch_mb = _dot(omb_ref[r, :], wbm_ref[...])
        merged = (jax.nn.sigmoid(_dot(xb, wgs_ref[...])) * branch_sb
                  + jax.nn.sigmoid(_dot(xb, wgm_ref[...])) * branch_mb)
        mix = _dot(_bf16(merged), wo_ref[...])
        o_ref[r, :] = _layer_norm(alpha * x + mix, g_ref[...], b_ref[...])


def _mix_block(x2, o_sb, o_mb, w_in, w_bsb, w_bmb, w_out, ln_g, ln_b, l, alpha, tm):
    t, d = x2.shape
    assert QKV_COLS % d == 0
    rows = lambda w: pl.BlockSpec((tm, w), lambda i: (i, 0))
    whole = lambda a: _layer_block(l, a.shape[1:])
    gate_cols = lambda j: _layer_block(l, (d, d), QKV_COLS // d + j)
    return pl.pallas_call(
        functools.partial(_mix_kernel, alpha=alpha),
        grid=(t // tm,),
        in_specs=[rows(d), rows(SB_WIDTH), rows(MOBA_WIDTH),
                  gate_cols(0), gate_cols(1), whole(w_bsb), whole(w_bmb), whole(w_out), whole(ln_g), whole(ln_b)],
        out_specs=rows(d),
        out_shape=jax.ShapeDtypeStruct((t, d), jnp.float32),
        compiler_params=_params(1),
        name="mix_ln",
    )(x2, o_sb, o_mb, w_in, w_in, w_bsb, w_bmb, w_out, ln_g, ln_b)


def _ffn_kernel(x_ref, wg_ref, wu_ref, wd_ref, g_ref, b_ref, o_ref, *, alpha, chunk):
    x = x_ref[...]
    xb = _bf16(x)
    acc = alpha * x
    for c in range(0, wg_ref.shape[1], chunk):
        gate = _dot(xb, wg_ref[:, c:c + chunk])
        up = _dot(xb, wu_ref[:, c:c + chunk])
        acc = acc + _dot(_bf16(jax.nn.silu(gate) * up), wd_ref[c:c + chunk, :])
    o_ref[...] = _layer_norm(acc, g_ref[...], b_ref[...])


def _ffn_chunk(d_ff):
    for c in range(MXU_WIDTH, d_ff, MXU_WIDTH):
        if d_ff % c == 0:
            return c
    return d_ff


def _ffn_block(x2, w_gate, w_up, w_down, ln_g, ln_b, l, alpha, tm):
    t, d = x2.shape
    rows = pl.BlockSpec((tm, d), lambda i: (i, 0))
    whole = lambda a: _layer_block(l, a.shape[1:])
    return pl.pallas_call(
        functools.partial(_ffn_kernel, alpha=alpha, chunk=_ffn_chunk(w_gate.shape[2])),
        grid=(t // tm,),
        in_specs=[rows, whole(w_gate), whole(w_up), whole(w_down), whole(ln_g), whole(ln_b)],
        out_specs=rows,
        out_shape=jax.ShapeDtypeStruct((t, d), jnp.float32),
        compiler_params=_params(1),
        name="ffn_ln",
    )(x2, w_gate, w_up, w_down, ln_g, ln_b)


def _rope_tables(seq):
    half = ROPE_DIM // 2
    inv_freq = ROPE_THETA ** (-jnp.arange(0, ROPE_DIM, 2, dtype=jnp.float32) / ROPE_DIM)
    dim = np.arange(LANES) % HEAD_DIM
    ang = jnp.arange(seq).astype(jnp.float32)[:, None] * inv_freq[dim % half][None, :]
    cos, sin = jnp.cos(ang), jnp.sin(ang)
    rotary = dim < ROPE_DIM
    return (jnp.where(rotary, cos, 1.0),
            jnp.where(rotary & (dim >= half), sin, 0.0),
            jnp.where(dim < half, -sin, 0.0))


def kernel(x, w_in, w_branch_sb, w_branch_moba, w_out, ln_mix_g, ln_mix_b,
           w_ffn_gate, w_ffn_up, w_ffn_down, ln_ffn_g, ln_ffn_b):
    batch, seq, d = x.shape
    depth = w_in.shape[0]
    assert w_in.shape[2] == QKV_COLS + 2 * d
    assert seq % MOBA_BLOCK == 0
    alpha = (2 * depth) ** 0.25
    tm = min(512, seq)
    cos_t, sina_t, sinb_t = _rope_tables(seq)
    x2 = x.reshape(batch * seq, d)
    w_in, w_bsb, w_bmb, w_o = _bf16(w_in), _bf16(w_branch_sb), _bf16(w_branch_moba), _bf16(w_out)
    w_fg, w_fu, w_fd = _bf16(w_ffn_gate), _bf16(w_ffn_up), _bf16(w_ffn_down)
    row = lambda a: a[:, None, :]
    for l in range(depth):
        qkv = _qkv_proj(x2, w_in, l, cos_t, sina_t, sinb_t, seq, tm)
        o_sb, o_mb = _attention(qkv, batch, seq)
        x2 = _mix_block(x2, o_sb, o_mb, w_in, w_bsb, w_bmb, w_o, row(ln_mix_g), row(ln_mix_b), l, alpha, tm)
        x2 = _ffn_block(x2, w_fg, w_fu, w_fd, row(ln_ffn_g), row(ln_ffn_b), l, alpha, tm)
    return x2.reshape(batch, seq, d)
```

```python
import functools

import jax
import jax.numpy as jnp
import numpy as np
from jax import lax
from jax.experimental import pallas as pl
from jax.experimental.pallas import tpu as pltpu

HEAD_DIM = 64
SB_HEADS = 8
MOBA_HEADS = 8
SB_WIDTH = SB_HEADS * HEAD_DIM
MOBA_WIDTH = MOBA_HEADS * HEAD_DIM
QKV_COLS = 3 * SB_WIDTH + 3 * MOBA_WIDTH
MOBA_BLOCK = 256
MOBA_TOPK = 3
ROPE_THETA = 500000.0
ROPE_DIM = HEAD_DIM // 4
LN_EPS = 1e-5
SCALE = HEAD_DIM ** -0.5

LANES = 128
HEADS_PER_TILE = LANES // HEAD_DIM
VMEM_LIMIT = 56 * 1024 * 1024
MXU_WIDTH = 256
MIX_CHUNKS = 2
PROJ_TN = 512
SB_TQ = 64
SB_WINDOW = 256
SB_STEP = 64
MOBA_GROUP = 4
ATTN_PAIRS = 2
MOBA_TAGS = 32
MOBA_SHIFT_MAX = 60.0
MOBA_NORM_SLACK = 1.01
NEG_BIG = -1e30
SB_UNDERFLOW = -104.0

_NT = (((1,), (1,)), ((), ()))


def _bf16(a):
    return a.astype(jnp.bfloat16)


def _dot(a, b):
    return jnp.dot(a, b, preferred_element_type=jnp.float32)


def _dot_nt(a, b):
    return lax.dot_general(a, b, _NT, preferred_element_type=jnp.float32)


def _split_dot(a_f32, tri2_ref, width):
    hi = _bf16(a_f32)
    lo = _bf16(a_f32 - hi.astype(jnp.float32))
    if width == SB_WINDOW:
        return _dot(jnp.concatenate([hi, lo], axis=1), tri2_ref[...])
    tri = tri2_ref[:width, :width]
    return _dot(jnp.concatenate([hi, lo], axis=1), jnp.concatenate([tri, tri], axis=0))


def _layer_norm(y, g, b):
    mu = jnp.mean(y, axis=-1, keepdims=True)
    d = y - mu
    var = jnp.mean(d * d, axis=-1, keepdims=True)
    return d * lax.rsqrt(var + LN_EPS) * g + b


def _params(n_grid):
    return pltpu.CompilerParams(dimension_semantics=("arbitrary",) * n_grid,
                                vmem_limit_bytes=VMEM_LIMIT)


def _qkv_kernel(x_ref, w_ref, cos_ref, sina_ref, sinb_ref, o_ref, *, rope_lo, rope_hi):
    xb = _bf16(x_ref[...])
    reps = PROJ_TN // LANES
    wide = lambda r: jnp.concatenate([r[...]] * reps, axis=1)
    for j in range(w_ref.shape[1] // PROJ_TN):
        cols = slice(j * PROJ_TN, (j + 1) * PROJ_TN)
        acc = _dot(xb, w_ref[:, cols])
        if rope_lo <= j < rope_hi:
            acc = (acc * wide(cos_ref)
                   + pltpu.roll(acc, ROPE_DIM // 2, 1) * wide(sina_ref)
                   + pltpu.roll(acc, PROJ_TN - ROPE_DIM // 2, 1) * wide(sinb_ref))
        o_ref[:, cols] = _bf16(acc)


def _layer_block(l, shape, col=0):
    return pl.BlockSpec((None,) + tuple(shape), lambda i: (l, 0, col), pipeline_mode=pl.Buffered(1))


def _qkv_proj(x2, w_in, l, cos_t, sina_t, sinb_t, seq, tm):
    t, d = x2.shape
    n = QKV_COLS
    s_tiles = seq // tm
    tab = pl.BlockSpec((tm, LANES), lambda i: (i % s_tiles, 0))
    rope_lo = 3 * SB_WIDTH // PROJ_TN
    rope_hi = (3 * SB_WIDTH + 2 * MOBA_WIDTH) // PROJ_TN
    return pl.pallas_call(
        functools.partial(_qkv_kernel, rope_lo=rope_lo, rope_hi=rope_hi),
        grid=(t // tm,),
        in_specs=[pl.BlockSpec((tm, d), lambda i: (i, 0)),
                  _layer_block(l, (d, n)), tab, tab, tab],
        out_specs=pl.BlockSpec((tm, n), lambda i: (i, 0)),
        out_shape=jax.ShapeDtypeStruct((t, n), jnp.bfloat16),
        compiler_params=_params(1),
        name="qkv_proj",
    )(x2, w_in, cos_t, sina_t, sinb_t)


def _sb_scores(k_ref, jobs):
    return [_dot_nt(q, k_ref[pl.ds(start, width), tile * LANES:(tile + 1) * LANES])
            for (q, tile, start, width, _, _) in jobs]


def _sb_logs(tri_ref, jobs, zs):
    log_keep, log_beta = [], []
    for (_, _, _, _, _, mask), z in zip(jobs, zs):
        lb = jnp.minimum(z, 0.0) - jnp.log(1.0 + jnp.exp(-jnp.abs(z)))
        lk = lb - z
        log_keep.append(lk if mask is None else jnp.where(mask, lk, 0.0))
        log_beta.append(lb)
    suffix = [_split_dot(lk, tri_ref, job[3]) for job, lk in zip(jobs, log_keep)]
    return log_keep, log_beta, suffix


def _sb_weights(v_ref, jobs, logs):
    log_keep, log_beta, suffix = logs
    out = []
    for u, (_, tile, start, width, (carry, acc), mask) in enumerate(jobs):
        w = jnp.exp(log_beta[u] + suffix[u] + carry)
        w = _bf16(w if mask is None else jnp.where(mask, w, 0.0))
        out.append((carry + suffix[u][:, :1] + log_keep[u][:, :1],
                    acc + _dot(w, v_ref[pl.ds(start, width), tile * LANES:(tile + 1) * LANES])))
    return out


def _attn_kernel(qs_ref, ks_ref, vs_ref, tri_ref, q_ref, k_ref, v_ref, os_ref, o_ref,
                 kmean_ref, knorm_ref, kaug_ref, vt_ref, *, n_blocks):
    i = pl.program_id(2)
    bs = MOBA_BLOCK
    hd = HEAD_DIM
    grp = MOBA_GROUP
    heads = range(ATTN_PAIRS * HEADS_PER_TILE)
    half_of = lambda h: h % HEADS_PER_TILE
    lane = lax.broadcasted_iota(jnp.int32, (bs, LANES), 1)

    tq, win, step = SB_TQ, SB_WINDOW, SB_STEP
    n_tiles = bs // tq
    sb_lane = lax.broadcasted_iota(jnp.int32, (tq, LANES), 1)
    sb_row = lax.broadcasted_iota(jnp.int32, (tq, win), 0)
    sb_col = lax.broadcasted_iota(jnp.int32, (tq, win), 1)
    zero = (jnp.zeros((tq, 1), jnp.float32), jnp.zeros((tq, LANES), jnp.float32))
    sb_jobs, sb_starts, sb_q = [], [], []
    for t in range(n_tiles):
        first_q = i * bs + t * tq
        start0 = pl.multiple_of(jnp.maximum(first_q + tq - win, 0), step)
        past = sb_col - sb_row < first_q - start0
        sb_starts.append(start0)
        for pr in range(ATTN_PAIRS):
            q = qs_ref[t * tq:(t + 1) * tq, pr * LANES:(pr + 1) * LANES]
            for h in range(HEADS_PER_TILE):
                qh = jnp.where(sb_lane // hd == h, q, jnp.zeros_like(q)) * SCALE
                sb_q.append(qh)
                sb_jobs.append((qh, pr, start0, win, zero, past))
    per_tile = ATTN_PAIRS * HEADS_PER_TILE

    @pl.when(i == 0)
    def _():
        kmean_ref[...] = jnp.zeros_like(kmean_ref)
        r_i = lax.broadcasted_iota(jnp.int32, (8 * HEADS_PER_TILE, LANES), 0)
        l_i = lax.broadcasted_iota(jnp.int32, (8 * HEADS_PER_TILE, LANES), 1)
        head_lanes = _bf16(r_i // 8 == l_i // hd)
        norm2 = [jnp.zeros((8 * HEADS_PER_TILE, bs), jnp.float32)] * ATTN_PAIRS
        for n in range(n_blocks):
            kbs = [k_ref[n * bs:(n + 1) * bs, pr * LANES:(pr + 1) * LANES].astype(jnp.float32)
                   for pr in range(ATTN_PAIRS)]
            for pr in range(ATTN_PAIRS):
                kmean_ref[pr, hd + n:hd + n + 1, :] = jnp.sum(kbs[pr], axis=0, keepdims=True) * (1.0 / bs)
                norm2[pr] = jnp.maximum(norm2[pr], _dot_nt(head_lanes, _bf16(kbs[pr] * kbs[pr])))
            tag = (lane == hd + n).astype(jnp.float32)
            for h in heads:
                kb = kbs[h // HEADS_PER_TILE]
                dims = kb if half_of(h) == 0 else pltpu.roll(kb, LANES - half_of(h) * hd, 1)
                kaug_ref[h, n * bs:(n + 1) * bs, :] = _bf16(jnp.where(lane < hd, dims, tag))
            vt_ref[n] = _bf16(v_ref[n * bs:(n + 1) * bs, :].astype(jnp.float32).T)
        for h in heads:
            per_key = norm2[h // HEADS_PER_TILE][8 * half_of(h):8 * half_of(h) + 1]
            knorm_ref[h] = jnp.broadcast_to(jnp.sqrt(jnp.max(per_key, axis=1, keepdims=True)), (1, LANES))

    qs_in = [q_ref[:, pr * LANES:(pr + 1) * LANES] for pr in range(ATTN_PAIRS)]
    qfs = [qq.astype(jnp.float32) for qq in qs_in]
    key = lax.broadcasted_iota(jnp.int32, (bs, bs), 0)
    qry = lax.broadcasted_iota(jnp.int32, (bs, bs), 1)
    rows = lax.broadcasted_iota(jnp.int32, (MOBA_TAGS, bs), 0)
    causal = key <= qry
    kmeans = [kmean_ref[pr, hd:hd + MOBA_TAGS, :] for pr in range(ATTN_PAIRS)]
    km_his = [_bf16(km) for km in kmeans]
    km_los = [_bf16(km - hi.astype(jnp.float32)) for km, hi in zip(kmeans, km_his)]
    own_start = pl.multiple_of(i * bs, bs)
    vt_own = vt_ref[i]

    dims, shift = [], []
    for h in heads:
        qf = qfs[h // HEADS_PER_TILE]
        d = (qf if half_of(h) == 0 else pltpu.roll(qf, LANES - half_of(h) * hd, 1)) * SCALE
        dims.append(d)
        norm = jnp.sqrt(jnp.sum(jnp.where(lane < hd, d * d, 0.0), axis=1, keepdims=True))
        shift.append(norm * knorm_ref[h] * MOBA_NORM_SLACK)

    def select():
        sel = []
        for h in heads:
            q = qs_in[h // HEADS_PER_TILE]
            km_hi, km_lo = km_his[h // HEADS_PER_TILE], km_los[h // HEADS_PER_TILE]
            qh = jnp.where(lane // hd == half_of(h), q, jnp.zeros_like(q))
            gate = _dot_nt(km_hi, qh) + _dot_nt(km_lo, qh)
            gate = jnp.where(rows < i, gate, -jnp.inf)
            chosen = jnp.zeros((MOBA_TAGS, bs), jnp.float32)
            for _ in range(MOBA_TOPK):
                best = jnp.max(gate, axis=0, keepdims=True)
                first = jnp.min(jnp.where(gate == best, rows, MOBA_TAGS), axis=0, keepdims=True)
                pick = (rows == first) & (best > -jnp.inf)
                chosen = jnp.where(pick, 1.0, chosen)
                gate = jnp.where(pick, -jnp.inf, gate)
            pad = jnp.zeros((hd, bs), jnp.float32)
            by_lane = jnp.concatenate([pad, chosen, jnp.zeros((LANES - hd - MOBA_TAGS, bs), jnp.float32)], axis=0).T
            sel.append(by_lane > 0.0)
        return sel

    def finish(state):
        out_t = jnp.concatenate([acc / l for (l, acc) in state], axis=0)
        return out_t.T

    def walk(q_past, step_fn, state, pair_groups):
        def scores(blk):
            start = pl.multiple_of(blk * bs, bs)
            return tuple(_dot_nt(kaug_ref[h, pl.ds(start, bs), :], q_past[h]) for h in heads)

        def visit(first_blk, n_blk, state):
            state = list(state)
            s_next = scores(first_blk) if n_blk else None
            for c in range(n_blk):
                s_cur, s_next = s_next, (scores(first_blk + c + 1) if c + 1 < n_blk else None)
                for h in heads:
                    state[h] = step_fn(state[h], s_cur[h], vt_ref[first_blk + c, h * hd:(h + 1) * hd, :])
            return tuple(state)

        if not pair_groups:
            return lax.fori_loop(0, (i + grp - 1) // grp, lambda g, st: visit(g * grp, grp, st), tuple(state))
        whole = i // grp
        state = lax.fori_loop(0, whole // 2, lambda g, st: visit(g * 2 * grp, 2 * grp, st), tuple(state))
        state = lax.fori_loop(whole // 2 * 2, whole, lambda g, st: visit(g * grp, grp, st), state)
        rest = [functools.partial(visit, whole * grp, r) for r in range(grp)]
        return lax.switch(i % grp, rest, state)

    def shifted():
        sb_z = _sb_scores(ks_ref, sb_jobs)
        sel = select()
        q_past = [_bf16(jnp.where(lane < hd, dims[h], jnp.where(sel[h], -shift[h], NEG_BIG))) for h in heads]
        own = [_dot_nt(kaug_ref[h, pl.ds(own_start, bs), :],
                       _bf16(jnp.where(lane < hd, dims[h], jnp.where(lane == hd + i, -shift[h], 0.0))))
               for h in heads]
        sb_logs = _sb_logs(tri_ref, sb_jobs, sb_z)
        state = []
        for h in heads:
            p = jnp.where(causal, jnp.exp(own[h]), 0.0)
            state.append((jnp.sum(p, axis=0, keepdims=True), _dot(vt_own[h * hd:(h + 1) * hd, :], _bf16(p))))
        sb_state = _sb_weights(vs_ref, sb_jobs, sb_logs)

        def step_fn(st, s, vt):
            p = jnp.exp(s)
            return st[0] + jnp.sum(p, axis=0, keepdims=True), st[1] + _dot(vt, _bf16(p))

        return finish(walk(q_past, step_fn, state, True)), tuple(sb_state)

    def running_max():
        sb_state = _sb_weights(vs_ref, sb_jobs, _sb_logs(tri_ref, sb_jobs, _sb_scores(ks_ref, sb_jobs)))
        sel = select()
        q_past, state = [], []
        for h in heads:
            q_past.append(_bf16(jnp.where(lane < hd, dims[h], jnp.where(sel[h], 0.0, NEG_BIG))))
            q_own = _bf16(jnp.where(lane < hd, dims[h], 0.0))
            s = jnp.where(causal, _dot_nt(kaug_ref[h, pl.ds(own_start, bs), :], q_own), NEG_BIG)
            m = jnp.max(s, axis=0, keepdims=True)
            p = jnp.exp(s - m)
            state.append((m, jnp.sum(p, axis=0, keepdims=True), _dot(vt_own[h * hd:(h + 1) * hd, :], _bf16(p))))

        def step_fn(st, s, vt):
            m_old, l_old, acc_old = st
            m_new = jnp.maximum(m_old, jnp.max(s, axis=0, keepdims=True))
            p = jnp.exp(s - m_new)
            alpha = jnp.exp(m_old - m_new)
            return m_new, alpha * l_old + jnp.sum(p, axis=0, keepdims=True), alpha * acc_old + _dot(vt, _bf16(p))

        return finish([st[1:] for st in walk(q_past, step_fn, state, False)]), tuple(sb_state)

    largest = jnp.max(functools.reduce(jnp.maximum, shift))
    moba_out, sb_state = lax.cond(2.0 * largest <= MOBA_SHIFT_MAX, shifted, running_max)
    o_ref[...] = _bf16(moba_out)

    def worst_bound(state, k):
        parts = []
        for t in range(n_tiles):
            both = functools.reduce(jnp.maximum, [state[t * per_tile + u][0] for u in range(per_tile)])
            parts.append(jnp.where(sb_starts[t] - k * step > 0, both, NEG_BIG))
        return jnp.max(functools.reduce(jnp.maximum, parts))

    def cond(loop):
        return loop[1] > SB_UNDERFLOW

    def body(loop):
        k, _, state = loop
        jobs = []
        for t in range(n_tiles):
            left = sb_starts[t] - k * step
            start = pl.multiple_of(jnp.maximum(left - step, 0), step)
            live = jnp.broadcast_to(left > 0, (tq, step))
            for u in range(per_tile):
                job = t * per_tile + u
                jobs.append((sb_q[job], sb_jobs[job][1], start, step, state[job], live))
        state = tuple(_sb_weights(vs_ref, jobs, _sb_logs(tri_ref, jobs, _sb_scores(ks_ref, jobs))))
        return k + 1, worst_bound(state, k + 1), state

    _, _, sb_state = lax.while_loop(cond, body, (0, worst_bound(sb_state, 0), sb_state))
    for t in range(n_tiles):
        for pr in range(ATTN_PAIRS):
            job = t * per_tile + pr * HEADS_PER_TILE
            os_ref[t * tq:(t + 1) * tq, pr * LANES:(pr + 1) * LANES] = _bf16(
                jnp.where(sb_lane < hd, sb_state[job][1], sb_state[job + 1][1]))


def _attention(qkv, batch, seq):
    t = qkv.shape[0]
    nq = seq // MOBA_BLOCK
    assert nq % MOBA_GROUP == 0, "the grouped block walk reads whole groups"
    assert nq <= MOBA_TAGS, "one block tag lane per key block"
    assert MOBA_BLOCK % SB_TQ == 0 and seq >= SB_WINDOW and SB_WIDTH == MOBA_WIDTH
    width = ATTN_PAIRS * LANES
    hp = MOBA_WIDTH // width
    n_heads = ATTN_PAIRS * HEADS_PER_TILE
    idx = jnp.arange(SB_WINDOW)
    tri = _bf16(idx[:, None] > idx[None, :])
    tri2 = jnp.concatenate([tri, tri], axis=0)
    rows_blk = lambda sec: pl.BlockSpec((MOBA_BLOCK, width), lambda b, p, i: (b * nq + i, sec * hp + p))
    seq_blk = lambda sec, bufs: pl.BlockSpec((seq, width), lambda b, p, i: (b, sec * hp + p),
                                             pipeline_mode=pl.Buffered(bufs))
    out_blk = pl.BlockSpec((MOBA_BLOCK, width), lambda b, p, i: (b * nq + i, p))
    return pl.pallas_call(
        functools.partial(_attn_kernel, n_blocks=nq),
        grid=(batch, hp, nq),
        in_specs=[rows_blk(0), seq_blk(1, 2), seq_blk(2, 2),
                  pl.BlockSpec((2 * SB_WINDOW, SB_WINDOW), lambda b, p, i: (0, 0)),
                  rows_blk(3), seq_blk(4, 1), seq_blk(5, 1)],
        out_specs=[out_blk, out_blk],
        out_shape=[jax.ShapeDtypeStruct((t, SB_WIDTH), jnp.bfloat16),
                   jax.ShapeDtypeStruct((t, MOBA_WIDTH), jnp.bfloat16)],
        scratch_shapes=[pltpu.VMEM((ATTN_PAIRS, LANES, LANES), jnp.float32),
                        pltpu.VMEM((n_heads, 1, LANES), jnp.float32),
                        pltpu.VMEM((n_heads, seq, LANES), jnp.bfloat16),
                        pltpu.VMEM((nq, width, MOBA_BLOCK), jnp.bfloat16)],
        compiler_params=_params(3),
        name="attention",
    )(qkv, qkv, qkv, tri2, qkv, qkv, qkv)


def _mix_kernel(x_ref, osb_ref, omb_ref, wgs_ref, wgm_ref, wbs_ref, wbm_ref, wo_ref, g_ref, b_ref, o_ref, *, alpha):
    rows = x_ref.shape[0] // MIX_CHUNKS
    for c in range(MIX_CHUNKS):
        r = slice(c * rows, (c + 1) * rows)
        x = x_ref[r, :]
        xb = _bf16(x)
        branch_sb = _dot(osb_ref[r, :], wbs_ref[...])
        branch_mb = _dot(omb_ref[r, :], wbm_ref[...])
        merged = (jax.nn.sigmoid(_dot(xb, wgs_ref[...])) * branch_sb
                  + jax.nn.sigmoid(_dot(xb, wgm_ref[...])) * branch_mb)
        mix = _dot(_bf16(merged), wo_ref[...])
        o_ref[r, :] = _layer_norm(alpha * x + mix, g_ref[...], b_ref[...])


def _mix_block(x2, o_sb, o_mb, w_in, w_bsb, w_bmb, w_out, ln_g, ln_b, l, alpha, tm):
    t, d = x2.shape
    assert QKV_COLS % d == 0
    rows = lambda w: pl.BlockSpec((tm, w), lambda i: (i, 0))
    whole = lambda a: _layer_block(l, a.shape[1:])
    gate_cols = lambda j: _layer_block(l, (d, d), QKV_COLS // d + j)
    return pl.pallas_call(
        functools.partial(_mix_kernel, alpha=alpha),
        grid=(t // tm,),
        in_specs=[rows(d), rows(SB_WIDTH), rows(MOBA_WIDTH),
                  gate_cols(0), gate_cols(1), whole(w_bsb), whole(w_bmb), whole(w_out), whole(ln_g), whole(ln_b)],
        out_specs=rows(d),
        out_shape=jax.ShapeDtypeStruct((t, d), jnp.float32),
        compiler_params=_params(1),
        name="mix_ln",
    )(x2, o_sb, o_mb, w_in, w_in, w_bsb, w_bmb, w_out, ln_g, ln_b)


def _ffn_kernel(x_ref, wg_ref, wu_ref, wd_ref, g_ref, b_ref, o_ref, *, alpha, chunk):
    x = x_ref[...]
    xb = _bf16(x)
    acc = alpha * x
    for c in range(0, wg_ref.shape[1], chunk):
        gate = _dot(xb, wg_ref[:, c:c + chunk])
        up = _dot(xb, wu_ref[:, c:c + chunk])
        acc = acc + _dot(_bf16(jax.nn.silu(gate) * up), wd_ref[c:c + chunk, :])
    o_ref[...] = _layer_norm(acc, g_ref[...], b_ref[...])


def _ffn_chunk(d_ff):
    for c in range(MXU_WIDTH, d_ff, MXU_WIDTH):
        if d_ff % c == 0:
            return c
    return d_ff


def _ffn_block(x2, w_gate, w_up, w_down, ln_g, ln_b, l, alpha, tm):
    t, d = x2.shape
    rows = pl.BlockSpec((tm, d), lambda i: (i, 0))
    whole = lambda a: _layer_block(l, a.shape[1:])
    return pl.pallas_call(
        functools.partial(_ffn_kernel, alpha=alpha, chunk=_ffn_chunk(w_gate.shape[2])),
        grid=(t // tm,),
        in_specs=[rows, whole(w_gate), whole(w_up), whole(w_down), whole(ln_g), whole(ln_b)],
        out_specs=rows,
        out_shape=jax.ShapeDtypeStruct((t, d), jnp.float32),
        compiler_params=_params(1),
        name="ffn_ln",
    )(x2, w_gate, w_up, w_down, ln_g, ln_b)


def _rope_tables(seq):
    half = ROPE_DIM // 2
    inv_freq = ROPE_THETA ** (-jnp.arange(0, ROPE_DIM, 2, dtype=jnp.float32) / ROPE_DIM)
    dim = np.arange(LANES) % HEAD_DIM
    ang = jnp.arange(seq).astype(jnp.float32)[:, None] * inv_freq[dim % half][None, :]
    cos, sin = jnp.cos(ang), jnp.sin(ang)
    rotary = dim < ROPE_DIM
    return (jnp.where(rotary, cos, 1.0),
            jnp.where(rotary & (dim >= half), sin, 0.0),
            jnp.where(dim < half, -sin, 0.0))


def kernel(x, w_in, w_branch_sb, w_branch_moba, w_out, ln_mix_g, ln_mix_b,
           w_ffn_gate, w_ffn_up, w_ffn_down, ln_ffn_g, ln_ffn_b):
    batch, seq, d = x.shape
    depth = w_in.shape[0]
    assert w_in.shape[2] == QKV_COLS + 2 * d
    assert seq % MOBA_BLOCK == 0
    alpha = (2 * depth) ** 0.25
    tm = min(512, seq)
    cos_t, sina_t, sinb_t = _rope_tables(seq)
    x2 = x.reshape(batch * seq, d)
    w_in, w_bsb, w_bmb, w_o = _bf16(w_in), _bf16(w_branch_sb), _bf16(w_branch_moba), _bf16(w_out)
    w_fg, w_fu, w_fd = _bf16(w_ffn_gate), _bf16(w_ffn_up), _bf16(w_ffn_down)
    row = lambda a: a[:, None, :]
    for l in range(depth):
        qkv = _qkv_proj(x2, w_in, l, cos_t, sina_t, sinb_t, seq, tm)
        o_sb, o_mb = _attention(qkv, batch, seq)
        x2 = _mix_block(x2, o_sb, o_mb, w_in, w_bsb, w_bmb, w_o, row(ln_mix_g), row(ln_mix_b), l, alpha, tm)
        x2 = _ffn_block(x2, w_fg, w_fu, w_fd, row(ln_ffn_g), row(ln_ffn_b), l, alpha, tm)
    return x2.reshape(batch, seq, d)
```

```python
import functools

import jax
import jax.numpy as jnp
import numpy as np
from jax import lax
from jax.experimental import pallas as pl
from jax.experimental.pallas import tpu as pltpu

HEAD_DIM = 64
SB_HEADS = 8
MOBA_HEADS = 8
SB_WIDTH = SB_HEADS * HEAD_DIM
MOBA_WIDTH = MOBA_HEADS * HEAD_DIM
QKV_COLS = 3 * SB_WIDTH + 3 * MOBA_WIDTH
MOBA_BLOCK = 256
MOBA_TOPK = 3
ROPE_THETA = 500000.0
ROPE_DIM = HEAD_DIM // 4
LN_EPS = 1e-5
SCALE = HEAD_DIM ** -0.5

LANES = 128
HEADS_PER_TILE = LANES // HEAD_DIM
VMEM_LIMIT = 56 * 1024 * 1024
MXU_WIDTH = 256
MIX_CHUNKS = 2
PROJ_TN = 512
SB_TQ = 64
SB_WINDOW = 256
SB_STEP = 64
MOBA_GROUP = 4
ATTN_PAIRS = 2
MOBA_TAGS = 32
MOBA_SHIFT_MAX = 60.0
MOBA_NORM_SLACK = 1.01
NEG_BIG = -1e30
SB_UNDERFLOW = -104.0

_NT = (((1,), (1,)), ((), ()))


def _bf16(a):
    return a.astype(jnp.bfloat16)


def _dot(a, b):
    return jnp.dot(a, b, preferred_element_type=jnp.float32)


def _dot_nt(a, b):
    return lax.dot_general(a, b, _NT, preferred_element_type=jnp.float32)


def _split_dot(a_f32, tri2_ref, width):
    hi = _bf16(a_f32)
    lo = _bf16(a_f32 - hi.astype(jnp.float32))
    if width == SB_WINDOW:
        return _dot(jnp.concatenate([hi, lo], axis=1), tri2_ref[...])
    tri = tri2_ref[:width, :width]
    return _dot(jnp.concatenate([hi, lo], axis=1), jnp.concatenate([tri, tri], axis=0))


def _layer_norm(y, g, b):
    mu = jnp.mean(y, axis=-1, keepdims=True)
    d = y - mu
    var = jnp.mean(d * d, axis=-1, keepdims=True)
    return d * lax.rsqrt(var + LN_EPS) * g + b


def _params(n_grid):
    return pltpu.CompilerParams(dimension_semantics=("arbitrary",) * n_grid,
                                vmem_limit_bytes=VMEM_LIMIT)


def _qkv_kernel(x_ref, w_ref, cos_ref, sina_ref, sinb_ref, o_ref, *, rope_lo, rope_hi):
    xb = _bf16(x_ref[...])
    reps = PROJ_TN // LANES
    wide = lambda r: jnp.concatenate([r[...]] * reps, axis=1)
    for j in range(w_ref.shape[1] // PROJ_TN):
        cols = slice(j * PROJ_TN, (j + 1) * PROJ_TN)
        acc = _dot(xb, w_ref[:, cols])
        if rope_lo <= j < rope_hi:
            acc = (acc * wide(cos_ref)
                   + pltpu.roll(acc, ROPE_DIM // 2, 1) * wide(sina_ref)
                   + pltpu.roll(acc, PROJ_TN - ROPE_DIM // 2, 1) * wide(sinb_ref))
        o_ref[:, cols] = _bf16(acc)


def _layer_block(l, shape, col=0):
    return pl.BlockSpec((None,) + tuple(shape), lambda i: (l, 0, col), pipeline_mode=pl.Buffered(1))


def _qkv_proj(x2, w_in, l, cos_t, sina_t, sinb_t, seq, tm):
    t, d = x2.shape
    n = QKV_COLS
    s_tiles = seq // tm
    tab = pl.BlockSpec((tm, LANES), lambda i: (i % s_tiles, 0))
    rope_lo = 3 * SB_WIDTH // PROJ_TN
    rope_hi = (3 * SB_WIDTH + 2 * MOBA_WIDTH) // PROJ_TN
    return pl.pallas_call(
        functools.partial(_qkv_kernel, rope_lo=rope_lo, rope_hi=rope_hi),
        grid=(t // tm,),
        in_specs=[pl.BlockSpec((tm, d), lambda i: (i, 0)),
                  _layer_block(l, (d, n)), tab, tab, tab],
        out_specs=pl.BlockSpec((tm, n), lambda i: (i, 0)),
        out_shape=jax.ShapeDtypeStruct((t, n), jnp.bfloat16),
        compiler_params=_params(1),
        name="qkv_proj",
    )(x2, w_in, cos_t, sina_t, sinb_t)


def _sb_scores(k_ref, jobs):
    return [_dot_nt(q, k_ref[pl.ds(start, width), tile * LANES:(tile + 1) * LANES])
            for (q, tile, start, width, _, _) in jobs]


def _sb_logs(tri_ref, jobs, zs):
    log_keep, log_beta = [], []
    for (_, _, _, _, _, mask), z in zip(jobs, zs):
        lb = jnp.minimum(z, 0.0) - jnp.log(1.0 + jnp.exp(-jnp.abs(z)))
        lk = lb - z
        log_keep.append(lk if mask is None else jnp.where(mask, lk, 0.0))
        log_beta.append(lb)
    suffix = [_split_dot(lk, tri_ref, job[3]) for job, lk in zip(jobs, log_keep)]
    return log_keep, log_beta, suffix


def _sb_weights(v_ref, jobs, logs):
    log_keep, log_beta, suffix = logs
    out = []
    for u, (_, tile, start, width, (carry, acc), mask) in enumerate(jobs):
        w = jnp.exp(log_beta[u] + suffix[u] + carry)
        w = _bf16(w if mask is None else jnp.where(mask, w, 0.0))
        out.append((carry + suffix[u][:, :1] + log_keep[u][:, :1],
                    acc + _dot(w, v_ref[pl.ds(start, width), tile * LANES:(tile + 1) * LANES])))
    return out


def _attn_kernel(qs_ref, ks_ref, vs_ref, tri_ref, q_ref, k_ref, v_ref, os_ref, o_ref,
                 kmean_ref, knorm_ref, kaug_ref, vt_ref, *, n_blocks):
    i = pl.program_id(2)
    bs = MOBA_BLOCK
    hd = HEAD_DIM
    grp = MOBA_GROUP
    heads = range(ATTN_PAIRS * HEADS_PER_TILE)
    half_of = lambda h: h % HEADS_PER_TILE
    lane = lax.broadcasted_iota(jnp.int32, (bs, LANES), 1)

    tq, win, step = SB_TQ, SB_WINDOW, SB_STEP
    n_tiles = bs // tq
    sb_lane = lax.broadcasted_iota(jnp.int32, (tq, LANES), 1)
    sb_row = lax.broadcasted_iota(jnp.int32, (tq, win), 0)
    sb_col = lax.broadcasted_iota(jnp.int32, (tq, win), 1)
    zero = (jnp.zeros((tq, 1), jnp.float32), jnp.zeros((tq, LANES), jnp.float32))
    sb_jobs, sb_starts, sb_q = [], [], []
    for t in range(n_tiles):
        first_q = i * bs + t * tq
        start0 = pl.multiple_of(jnp.maximum(first_q + tq - win, 0), step)
        past = sb_col - sb_row < first_q - start0
        sb_starts.append(start0)
        for pr in range(ATTN_PAIRS):
            q = qs_ref[t * tq:(t + 1) * tq, pr * LANES:(pr + 1) * LANES]
            for h in range(HEADS_PER_TILE):
                qh = jnp.where(sb_lane // hd == h, q, jnp.zeros_like(q)) * SCALE
                sb_q.append(qh)
                sb_jobs.append((qh, pr, start0, win, zero, past))
    per_tile = ATTN_PAIRS * HEADS_PER_TILE

    @pl.when(i == 0)
    def _():
        kmean_ref[...] = jnp.zeros_like(kmean_ref)
        r_i = lax.broadcasted_iota(jnp.int32, (8 * HEADS_PER_TILE, LANES), 0)
        l_i = lax.broadcasted_iota(jnp.int32, (8 * HEADS_PER_TILE, LANES), 1)
        head_lanes = _bf16(r_i // 8 == l_i // hd)
        norm2 = [jnp.zeros((8 * HEADS_PER_TILE, bs), jnp.float32)] * ATTN_PAIRS
        for n in range(n_blocks):
            kbs = [k_ref[n * bs:(n + 1) * bs, pr * LANES:(pr + 1) * LANES].astype(jnp.float32)
                   for pr in range(ATTN_PAIRS)]
            for pr in range(ATTN_PAIRS):
                kmean_ref[pr, hd + n:hd + n + 1, :] = jnp.sum(kbs[pr], axis=0, keepdims=True) * (1.0 / bs)
                norm2[pr] = jnp.maximum(norm2[pr], _dot_nt(head_lanes, _bf16(kbs[pr] * kbs[pr])))
            tag = (lane == hd + n).astype(jnp.float32)
            for h in heads:
                kb = kbs[h // HEADS_PER_TILE]
                dims = kb if half_of(h) == 0 else pltpu.roll(kb, LANES - half_of(h) * hd, 1)
                kaug_ref[h, n * bs:(n + 1) * bs, :] = _bf16(jnp.where(lane < hd, dims, tag))
            vt_ref[n] = _bf16(v_ref[n * bs:(n + 1) * bs, :].astype(jnp.float32).T)
        for h in heads:
            per_key = norm2[h // HEADS_PER_TILE][8 * half_of(h):8 * half_of(h) + 1]
            knorm_ref[h] = jnp.broadcast_to(jnp.sqrt(jnp.max(per_key, axis=1, keepdims=True)), (1, LANES))

    qs_in = [q_ref[:, pr * LANES:(pr + 1) * LANES] for pr in range(ATTN_PAIRS)]
    qfs = [qq.astype(jnp.float32) for qq in qs_in]
    key = lax.broadcasted_iota(jnp.int32, (bs, bs), 0)
    qry = lax.broadcasted_iota(jnp.int32, (bs, bs), 1)
    rows = lax.broadcasted_iota(jnp.int32, (MOBA_TAGS, bs), 0)
    causal = key <= qry
    kmeans = [kmean_ref[pr, hd:hd + MOBA_TAGS, :] for pr in range(ATTN_PAIRS)]
    km_his = [_bf16(km) for km in kmeans]
    km_los = [_bf16(km - hi.astype(jnp.float32)) for km, hi in zip(kmeans, km_his)]
    own_start = pl.multiple_of(i * bs, bs)
    vt_own = vt_ref[i]

    dims, shift = [], []
    for h in heads:
        qf = qfs[h // HEADS_PER_TILE]
        d = (qf if half_of(h) == 0 else pltpu.roll(qf, LANES - half_of(h) * hd, 1)) * SCALE
        dims.append(d)
        norm = jnp.sqrt(jnp.sum(jnp.where(lane < hd, d * d, 0.0), axis=1, keepdims=True))
        shift.append(norm * knorm_ref[h] * MOBA_NORM_SLACK)

    def select():
        sel = []
        for h in heads:
            q = qs_in[h // HEADS_PER_TILE]
            km_hi, km_lo = km_his[h // HEADS_PER_TILE], km_los[h // HEADS_PER_TILE]
            qh = jnp.where(lane // hd == half_of(h), q, jnp.zeros_like(q))
            gate = _dot_nt(km_hi, qh) + _dot_nt(km_lo, qh)
            gate = jnp.where(rows < i, gate, -jnp.inf)
            chosen = jnp.zeros((MOBA_TAGS, bs), jnp.float32)
            for _ in range(MOBA_TOPK):
                best = jnp.max(gate, axis=0, keepdims=True)
                first = jnp.min(jnp.where(gate == best, rows, MOBA_TAGS), axis=0, keepdims=True)
                pick = (rows == first) & (best > -jnp.inf)
                chosen = jnp.where(pick, 1.0, chosen)
                gate = jnp.where(pick, -jnp.inf, gate)
            pad = jnp.zeros((hd, bs), jnp.float32)
            by_lane = jnp.concatenate([pad, chosen, jnp.zeros((LANES - hd - MOBA_TAGS, bs), jnp.float32)], axis=0).T
            sel.append(by_lane > 0.0)
        return sel

    def sb_bound(state, k):
        parts = []
        for t in range(n_tiles):
            both = functools.reduce(jnp.maximum, [state[t * per_tile + u][0] for u in range(per_tile)])
            parts.append(jnp.where(sb_starts[t] - k * step > 0, both, NEG_BIG))
        return jnp.max(functools.reduce(jnp.maximum, parts))

    def finish(state):
        out_t = jnp.concatenate([acc / l for (l, acc) in state], axis=0)
        return out_t.T

    def walk(q_past, step_fn, state, pair_groups):
        def scores(blk):
            start = pl.multiple_of(blk * bs, bs)
            return tuple(_dot_nt(kaug_ref[h, pl.ds(start, bs), :], q_past[h]) for h in heads)

        def visit(first_blk, n_blk, state):
            state = list(state)
            s_next = scores(first_blk) if n_blk else None
            for c in range(n_blk):
                s_cur, s_next = s_next, (scores(first_blk + c + 1) if c + 1 < n_blk else None)
                for h in heads:
                    state[h] = step_fn(state[h], s_cur[h], vt_ref[first_blk + c, h * hd:(h + 1) * hd, :])
            return tuple(state)

        if not pair_groups:
            return lax.fori_loop(0, (i + grp - 1) // grp, lambda g, st: visit(g * grp, grp, st), tuple(state))
        whole = i // grp
        state = lax.fori_loop(0, whole // 2, lambda g, st: visit(g * 2 * grp, 2 * grp, st), tuple(state))
        state = lax.fori_loop(whole // 2 * 2, whole, lambda g, st: visit(g * grp, grp, st), state)
        rest = [functools.partial(visit, whole * grp, r) for r in range(grp)]
        return lax.switch(i % grp, rest, state)

    def shifted():
        sel = select()
        q_past = [_bf16(jnp.where(lane < hd, dims[h], jnp.where(sel[h], -shift[h], NEG_BIG))) for h in heads]
        own = [_dot_nt(kaug_ref[h, pl.ds(own_start, bs), :],
                       _bf16(jnp.where(lane < hd, dims[h], jnp.where(lane == hd + i, -shift[h], 0.0))))
               for h in heads]
        state = []
        for h in heads:
            p = jnp.where(causal, jnp.exp(own[h]), 0.0)
            state.append((jnp.sum(p, axis=0, keepdims=True), _dot(vt_own[h * hd:(h + 1) * hd, :], _bf16(p))))
        sb_state = tuple(_sb_weights(vs_ref, sb_jobs, sb_logs))
        sb_worst = sb_bound(sb_state, 0)

        def step_fn(st, s, vt):
            p = jnp.exp(s)
            return st[0] + jnp.sum(p, axis=0, keepdims=True), st[1] + _dot(vt, _bf16(p))

        return finish(walk(q_past, step_fn, state, True)), sb_state, sb_worst

    def running_max():
        sb_state = tuple(_sb_weights(vs_ref, sb_jobs, sb_logs))
        sb_worst = sb_bound(sb_state, 0)
        sel = select()
        q_past, state = [], []
        for h in heads:
            q_past.append(_bf16(jnp.where(lane < hd, dims[h], jnp.where(sel[h], 0.0, NEG_BIG))))
            q_own = _bf16(jnp.where(lane < hd, dims[h], 0.0))
            s = jnp.where(causal, _dot_nt(kaug_ref[h, pl.ds(own_start, bs), :], q_own), NEG_BIG)
            m = jnp.max(s, axis=0, keepdims=True)
            p = jnp.exp(s - m)
            state.append((m, jnp.sum(p, axis=0, keepdims=True), _dot(vt_own[h * hd:(h + 1) * hd, :], _bf16(p))))

        def step_fn(st, s, vt):
            m_old, l_old, acc_old = st
            m_new = jnp.maximum(m_old, jnp.max(s, axis=0, keepdims=True))
            p = jnp.exp(s - m_new)
            alpha = jnp.exp(m_old - m_new)
            return m_new, alpha * l_old + jnp.sum(p, axis=0, keepdims=True), alpha * acc_old + _dot(vt, _bf16(p))

        return finish([st[1:] for st in walk(q_past, step_fn, state, False)]), sb_state, sb_worst

    largest = jnp.max(functools.reduce(jnp.maximum, shift))
    sb_logs = _sb_logs(tri_ref, sb_jobs, _sb_scores(ks_ref, sb_jobs))
    moba_out, sb_state, sb_worst = lax.cond(2.0 * largest <= MOBA_SHIFT_MAX, shifted, running_max)
    o_ref[...] = _bf16(moba_out)

    def cond(loop):
        return loop[1] > SB_UNDERFLOW

    def body(loop):
        k, _, state = loop
        jobs = []
        for t in range(n_tiles):
            left = sb_starts[t] - k * step
            start = pl.multiple_of(jnp.maximum(left - step, 0), step)
            live = jnp.broadcast_to(left > 0, (tq, step))
            for u in range(per_tile):
                job = t * per_tile + u
                jobs.append((sb_q[job], sb_jobs[job][1], start, step, state[job], live))
        state = tuple(_sb_weights(vs_ref, jobs, _sb_logs(tri_ref, jobs, _sb_scores(ks_ref, jobs))))
        return k + 1, sb_bound(state, k + 1), state

    _, _, sb_state = lax.while_loop(cond, body, (0, sb_worst, sb_state))
    for t in range(n_tiles):
        for pr in range(ATTN_PAIRS):
            job = t * per_tile + pr * HEADS_PER_TILE
            os_ref[t * tq:(t + 1) * tq, pr * LANES:(pr + 1) * LANES] = _bf16(
                jnp.where(sb_lane < hd, sb_state[job][1], sb_state[job + 1][1]))


def _attention(qkv, batch, seq):
    t = qkv.shape[0]
    nq = seq // MOBA_BLOCK
    assert nq % MOBA_GROUP == 0, "the grouped block walk reads whole groups"
    assert nq <= MOBA_TAGS, "one block tag lane per key block"
    assert MOBA_BLOCK % SB_TQ == 0 and seq >= SB_WINDOW and SB_WIDTH == MOBA_WIDTH
    width = ATTN_PAIRS * LANES
    hp = MOBA_WIDTH // width
    n_heads = ATTN_PAIRS * HEADS_PER_TILE
    idx = jnp.arange(SB_WINDOW)
    tri = _bf16(idx[:, None] > idx[None, :])
    tri2 = jnp.concatenate([tri, tri], axis=0)
    rows_blk = lambda sec: pl.BlockSpec((MOBA_BLOCK, width), lambda b, p, i: (b * nq + i, sec * hp + p))
    seq_blk = lambda sec, bufs: pl.BlockSpec((seq, width), lambda b, p, i: (b, sec * hp + p),
                                             pipeline_mode=pl.Buffered(bufs))
    out_blk = pl.BlockSpec((MOBA_BLOCK, width), lambda b, p, i: (b * nq + i, p))
    return pl.pallas_call(
        functools.partial(_attn_kernel, n_blocks=nq),
        grid=(batch, hp, nq),
        in_specs=[rows_blk(0), seq_blk(1, 2), seq_blk(2, 2),
                  pl.BlockSpec((2 * SB_WINDOW, SB_WINDOW), lambda b, p, i: (0, 0)),
                  rows_blk(3), seq_blk(4, 1), seq_blk(5, 1)],
        out_specs=[out_blk, out_blk],
        out_shape=[jax.ShapeDtypeStruct((t, SB_WIDTH), jnp.bfloat16),
                   jax.ShapeDtypeStruct((t, MOBA_WIDTH), jnp.bfloat16)],
        scratch_shapes=[pltpu.VMEM((ATTN_PAIRS, LANES, LANES), jnp.float32),
                        pltpu.VMEM((n_heads, 1, LANES), jnp.float32),
                        pltpu.VMEM((n_heads, seq, LANES), jnp.bfloat16),
                        pltpu.VMEM((nq, width, MOBA_BLOCK), jnp.bfloat16)],
        compiler_params=_params(3),
        name="attention",
    )(qkv, qkv, qkv, tri2, qkv, qkv, qkv)


def _mix_kernel(x_ref, osb_ref, omb_ref, wgs_ref, wgm_ref, wbs_ref, wbm_ref, wo_ref, g_ref, b_ref, o_ref, *, alpha):
    rows = x_ref.shape[0] // MIX_CHUNKS
    for c in range(MIX_CHUNKS):
        r = slice(c * rows, (c + 1) * rows)
        x = x_ref[r, :]
        xb = _bf16(x)
        branch_sb = _dot(osb_ref[r, :], wbs_ref[...])
        branch_mb = _dot(omb_ref[r, :], wbm_ref[...])
        merged = (jax.nn.sigmoid(_dot(xb, wgs_ref[...])) * branch_sb
                  + jax.nn.sigmoid(_dot(xb, wgm_ref[...])) * branch_mb)
        mix = _dot(_bf16(merged), wo_ref[...])
        o_ref[r, :] = _layer_norm(alpha * x + mix, g_ref[...], b_ref[...])


def _mix_block(x2, o_sb, o_mb, w_in, w_bsb, w_bmb, w_out, ln_g, ln_b, l, alpha, tm):
    t, d = x2.shape
    assert QKV_COLS % d == 0
    rows = lambda w: pl.BlockSpec((tm, w), lambda i: (i, 0))
    whole = lambda a: _layer_block(l, a.shape[1:])
    gate_cols = lambda j: _layer_block(l, (d, d), QKV_COLS // d + j)
    return pl.pallas_call(
        functools.partial(_mix_kernel, alpha=alpha),
        grid=(t // tm,),
        in_specs=[rows(d), rows(SB_WIDTH), rows(MOBA_WIDTH),
                  gate_cols(0), gate_cols(1), whole(w_bsb), whole(w_bmb), whole(w_out), whole(ln_g), whole(ln_b)],
        out_specs=rows(d),
        out_shape=jax.ShapeDtypeStruct((t, d), jnp.float32),
        compiler_params=_params(1),
        name="mix_ln",
    )(x2, o_sb, o_mb, w_in, w_in, w_bsb, w_bmb, w_out, ln_g, ln_b)


def _ffn_kernel(x_ref, wg_ref, wu_ref, wd_ref, g_ref, b_ref, o_ref, *, alpha, chunk):
    x = x_ref[...]
    xb = _bf16(x)
    acc = alpha * x
    for c in range(0, wg_ref.shape[1], chunk):
        gate = _dot(xb, wg_ref[:, c:c + chunk])
        up = _dot(xb, wu_ref[:, c:c + chunk])
        acc = acc + _dot(_bf16(jax.nn.silu(gate) * up), wd_ref[c:c + chunk, :])
    o_ref[...] = _layer_norm(acc, g_ref[...], b_ref[...])


def _ffn_chunk(d_ff):
    for c in range(MXU_WIDTH, d_ff, MXU_WIDTH):
        if d_ff % c == 0:
            return c
    return d_ff


def _ffn_block(x2, w_gate, w_up, w_down, ln_g, ln_b, l, alpha, tm):
    t, d = x2.shape
    rows = pl.BlockSpec((tm, d), lambda i: (i, 0))
    whole = lambda a: _layer_block(l, a.shape[1:])
    return pl.pallas_call(
        functools.partial(_ffn_kernel, alpha=alpha, chunk=_ffn_chunk(w_gate.shape[2])),
        grid=(t // tm,),
        in_specs=[rows, whole(w_gate), whole(w_up), whole(w_down), whole(ln_g), whole(ln_b)],
        out_specs=rows,
        out_shape=jax.ShapeDtypeStruct((t, d), jnp.float32),
        compiler_params=_params(1),
        name="ffn_ln",
    )(x2, w_gate, w_up, w_down, ln_g, ln_b)


def _rope_tables(seq):
    half = ROPE_DIM // 2
    inv_freq = ROPE_THETA ** (-jnp.arange(0, ROPE_DIM, 2, dtype=jnp.float32) / ROPE_DIM)
    dim = np.arange(LANES) % HEAD_DIM
    ang = jnp.arange(seq).astype(jnp.float32)[:, None] * inv_freq[dim % half][None, :]
    cos, sin = jnp.cos(ang), jnp.sin(ang)
    rotary = dim < ROPE_DIM
    return (jnp.where(rotary, cos, 1.0),
            jnp.where(rotary & (dim >= half), sin, 0.0),
            jnp.where(dim < half, -sin, 0.0))


def kernel(x, w_in, w_branch_sb, w_branch_moba, w_out, ln_mix_g, ln_mix_b,
           w_ffn_gate, w_ffn_up, w_ffn_down, ln_ffn_g, ln_ffn_b):
    batch, seq, d = x.shape
    depth = w_in.shape[0]
    assert w_in.shape[2] == QKV_COLS + 2 * d
    assert seq % MOBA_BLOCK == 0
    alpha = (2 * depth) ** 0.25
    tm = min(512, seq)
    cos_t, sina_t, sinb_t = _rope_tables(seq)
    x2 = x.reshape(batch * seq, d)
    w_in, w_bsb, w_bmb, w_o = _bf16(w_in), _bf16(w_branch_sb), _bf16(w_branch_moba), _bf16(w_out)
    w_fg, w_fu, w_fd = _bf16(w_ffn_gate), _bf16(w_ffn_up), _bf16(w_ffn_down)
    row = lambda a: a[:, None, :]
    for l in range(depth):
        qkv = _qkv_proj(x2, w_in, l, cos_t, sina_t, sinb_t, seq, tm)
        o_sb, o_mb = _attention(qkv, batch, seq)
        x2 = _mix_block(x2, o_sb, o_mb, w_in, w_bsb, w_bmb, w_o, row(ln_mix_g), row(ln_mix_b), l, alpha, tm)
        x2 = _ffn_block(x2, w_fg, w_fu, w_fd, row(ln_ffn_g), row(ln_ffn_b), l, alpha, tm)
    return x2.reshape(batch, seq, d)
```

```python
import functools

import jax
import jax.numpy as jnp
import numpy as np
from jax import lax
from jax.experimental import pallas as pl
from jax.experimental.pallas import tpu as pltpu

HEAD_DIM = 64
SB_HEADS = 8
MOBA_HEADS = 8
SB_WIDTH = SB_HEADS * HEAD_DIM
MOBA_WIDTH = MOBA_HEADS * HEAD_DIM
QKV_COLS = 3 * SB_WIDTH + 3 * MOBA_WIDTH
MOBA_BLOCK = 256
MOBA_TOPK = 3
ROPE_THETA = 500000.0
ROPE_DIM = HEAD_DIM // 4
LN_EPS = 1e-5
SCALE = HEAD_DIM ** -0.5

LANES = 128
HEADS_PER_TILE = LANES // HEAD_DIM
VMEM_LIMIT = 56 * 1024 * 1024
MXU_WIDTH = 256
MIX_CHUNKS = 4
PROJ_TN = 512
SB_TQ = 64
SB_WINDOW = 256
SB_STEP = 64
MOBA_GROUP = 4
ATTN_PAIRS = 2
ATTN_QBLOCKS = 2
MOBA_TAGS = 32
MOBA_SHIFT_MAX = 60.0
MOBA_NORM_SLACK = 1.01
NEG_BIG = -1e30
SB_UNDERFLOW = -104.0

_NT = (((1,), (1,)), ((), ()))


def _bf16(a):
    return a.astype(jnp.bfloat16)


def _dot(a, b):
    return jnp.dot(a, b, preferred_element_type=jnp.float32)


def _dot_nt(a, b):
    return lax.dot_general(a, b, _NT, preferred_element_type=jnp.float32)


def _split_dot(a_f32, tri2_ref, width):
    hi = _bf16(a_f32)
    lo = _bf16(a_f32 - hi.astype(jnp.float32))
    if width == SB_WINDOW:
        return _dot(jnp.concatenate([hi, lo], axis=1), tri2_ref[...])
    tri = tri2_ref[:width, :width]
    return _dot(jnp.concatenate([hi, lo], axis=1), jnp.concatenate([tri, tri], axis=0))


def _layer_norm(y, g, b):
    mu = jnp.mean(y, axis=-1, keepdims=True)
    d = y - mu
    var = jnp.mean(d * d, axis=-1, keepdims=True)
    return d * lax.rsqrt(var + LN_EPS) * g + b


def _params(n_grid):
    return pltpu.CompilerParams(dimension_semantics=("arbitrary",) * n_grid,
                                vmem_limit_bytes=VMEM_LIMIT)


def _qkv_kernel(x_ref, w_ref, cos_ref, sina_ref, sinb_ref, o_ref, *, rope_lo, rope_hi):
    xb = _bf16(x_ref[...])
    reps = PROJ_TN // LANES
    wide = lambda r: jnp.concatenate([r[...]] * reps, axis=1)
    for j in range(w_ref.shape[1] // PROJ_TN):
        cols = slice(j * PROJ_TN, (j + 1) * PROJ_TN)
        acc = _dot(xb, w_ref[:, cols])
        if rope_lo <= j < rope_hi:
            acc = (acc * wide(cos_ref)
                   + pltpu.roll(acc, ROPE_DIM // 2, 1) * wide(sina_ref)
                   + pltpu.roll(acc, PROJ_TN - ROPE_DIM // 2, 1) * wide(sinb_ref))
        o_ref[:, cols] = _bf16(acc)


def _layer_block(l, shape, col=0):
    return pl.BlockSpec((None,) + tuple(shape), lambda i: (l, 0, col), pipeline_mode=pl.Buffered(1))


def _qkv_proj(x2, w_in, l, cos_t, sina_t, sinb_t, seq, tm):
    t, d = x2.shape
    n = QKV_COLS
    s_tiles = seq // tm
    tab = pl.BlockSpec((tm, LANES), lambda i: (i % s_tiles, 0))
    rope_lo = 3 * SB_WIDTH // PROJ_TN
    rope_hi = (3 * SB_WIDTH + 2 * MOBA_WIDTH) // PROJ_TN
    return pl.pallas_call(
        functools.partial(_qkv_kernel, rope_lo=rope_lo, rope_hi=rope_hi),
        grid=(t // tm,),
        in_specs=[pl.BlockSpec((tm, d), lambda i: (i, 0)),
                  _layer_block(l, (d, n)), tab, tab, tab],
        out_specs=pl.BlockSpec((tm, n), lambda i: (i, 0)),
        out_shape=jax.ShapeDtypeStruct((t, n), jnp.bfloat16),
        compiler_params=_params(1),
        name="qkv_proj",
    )(x2, w_in, cos_t, sina_t, sinb_t)


def _sb_scores(k_ref, jobs):
    return [_dot_nt(q, k_ref[pl.ds(start, width), tile * LANES:(tile + 1) * LANES])
            for (q, tile, start, width, _, _) in jobs]


def _sb_logs(tri_ref, jobs, zs):
    log_keep, log_beta = [], []
    for (_, _, _, _, _, mask), z in zip(jobs, zs):
        lb = jnp.minimum(z, 0.0) - jnp.log(1.0 + jnp.exp(-jnp.abs(z)))
        lk = lb - z
        log_keep.append(lk if mask is None else jnp.where(mask, lk, 0.0))
        log_beta.append(lb)
    suffix = [_split_dot(lk, tri_ref, job[3]) for job, lk in zip(jobs, log_keep)]
    return log_keep, log_beta, suffix


def _sb_weights(v_ref, jobs, logs):
    log_keep, log_beta, suffix = logs
    out = []
    for u, (_, tile, start, width, (carry, acc), mask) in enumerate(jobs):
        w = jnp.exp(log_beta[u] + suffix[u] + carry)
        w = _bf16(w if mask is None else jnp.where(mask, w, 0.0))
        out.append((carry + suffix[u][:, :1] + log_keep[u][:, :1],
                    acc + _dot(w, v_ref[pl.ds(start, width), tile * LANES:(tile + 1) * LANES])))
    return out


def _attn_kernel(qs_ref, ks_ref, vs_ref, tri_ref, q_ref, k_ref, v_ref, os_ref, o_ref,
                 kmean_ref, knorm_ref, kaug_ref, vt_ref, *, n_blocks):
    step_id = pl.program_id(2)
    bs = MOBA_BLOCK
    hd = HEAD_DIM
    grp = MOBA_GROUP
    heads = range(ATTN_PAIRS * HEADS_PER_TILE)
    half_of = lambda h: h % HEADS_PER_TILE
    lane = lax.broadcasted_iota(jnp.int32, (bs, LANES), 1)

    @pl.when(step_id == 0)
    def _():
        kmean_ref[...] = jnp.zeros_like(kmean_ref)
        r_i = lax.broadcasted_iota(jnp.int32, (8 * HEADS_PER_TILE, LANES), 0)
        l_i = lax.broadcasted_iota(jnp.int32, (8 * HEADS_PER_TILE, LANES), 1)
        head_lanes = _bf16(r_i // 8 == l_i // hd)
        norm2 = [jnp.zeros((8 * HEADS_PER_TILE, bs), jnp.float32)] * ATTN_PAIRS
        for n in range(n_blocks):
            kbs = [k_ref[n * bs:(n + 1) * bs, pr * LANES:(pr + 1) * LANES].astype(jnp.float32)
                   for pr in range(ATTN_PAIRS)]
            for pr in range(ATTN_PAIRS):
                kmean_ref[pr, hd + n:hd + n + 1, :] = jnp.sum(kbs[pr], axis=0, keepdims=True) * (1.0 / bs)
                norm2[pr] = jnp.maximum(norm2[pr], _dot_nt(head_lanes, _bf16(kbs[pr] * kbs[pr])))
            tag = (lane == hd + n).astype(jnp.float32)
            for h in heads:
                kb = kbs[h // HEADS_PER_TILE]
                dims = kb if half_of(h) == 0 else pltpu.roll(kb, LANES - half_of(h) * hd, 1)
                kaug_ref[h, n * bs:(n + 1) * bs, :] = _bf16(jnp.where(lane < hd, dims, tag))
            vt_ref[n] = _bf16(v_ref[n * bs:(n + 1) * bs, :].astype(jnp.float32).T)
        for h in heads:
            per_key = norm2[h // HEADS_PER_TILE][8 * half_of(h):8 * half_of(h) + 1]
            knorm_ref[h] = jnp.broadcast_to(jnp.sqrt(jnp.max(per_key, axis=1, keepdims=True)), (1, LANES))

    def block(i, r0):
        tq, win, step = SB_TQ, SB_WINDOW, SB_STEP
        n_tiles = bs // tq
        sb_lane = lax.broadcasted_iota(jnp.int32, (tq, LANES), 1)
        sb_row = lax.broadcasted_iota(jnp.int32, (tq, win), 0)
        sb_col = lax.broadcasted_iota(jnp.int32, (tq, win), 1)
        zero = (jnp.zeros((tq, 1), jnp.float32), jnp.zeros((tq, LANES), jnp.float32))
        sb_jobs, sb_starts, sb_q = [], [], []
        for t in range(n_tiles):
            first_q = i * bs + t * tq
            start0 = pl.multiple_of(jnp.maximum(first_q + tq - win, 0), step)
            past = sb_col - sb_row < first_q - start0
            sb_starts.append(start0)
            for pr in range(ATTN_PAIRS):
                q = qs_ref[r0 + t * tq:r0 + (t + 1) * tq, pr * LANES:(pr + 1) * LANES]
                for h in range(HEADS_PER_TILE):
                    qh = jnp.where(sb_lane // hd == h, q, jnp.zeros_like(q)) * SCALE
                    sb_q.append(qh)
                    sb_jobs.append((qh, pr, start0, win, zero, past))
        per_tile = ATTN_PAIRS * HEADS_PER_TILE

        qs_in = [q_ref[r0:r0 + bs, pr * LANES:(pr + 1) * LANES] for pr in range(ATTN_PAIRS)]
        qfs = [qq.astype(jnp.float32) for qq in qs_in]
        key = lax.broadcasted_iota(jnp.int32, (bs, bs), 0)
        qry = lax.broadcasted_iota(jnp.int32, (bs, bs), 1)
        rows = lax.broadcasted_iota(jnp.int32, (MOBA_TAGS, bs), 0)
        causal = key <= qry
        kmeans = [kmean_ref[pr, hd:hd + MOBA_TAGS, :] for pr in range(ATTN_PAIRS)]
        km_his = [_bf16(km) for km in kmeans]
        km_los = [_bf16(km - hi.astype(jnp.float32)) for km, hi in zip(kmeans, km_his)]
        own_start = pl.multiple_of(i * bs, bs)
        vt_own = vt_ref[i]

        dims, shift = [], []
        for h in heads:
            qf = qfs[h // HEADS_PER_TILE]
            d = (qf if half_of(h) == 0 else pltpu.roll(qf, LANES - half_of(h) * hd, 1)) * SCALE
            dims.append(d)
            norm = jnp.sqrt(jnp.sum(jnp.where(lane < hd, d * d, 0.0), axis=1, keepdims=True))
            shift.append(norm * knorm_ref[h] * MOBA_NORM_SLACK)

        def select():
            sel = []
            for h in heads:
                q = qs_in[h // HEADS_PER_TILE]
                km_hi, km_lo = km_his[h // HEADS_PER_TILE], km_los[h // HEADS_PER_TILE]
                qh = jnp.where(lane // hd == half_of(h), q, jnp.zeros_like(q))
                gate = _dot_nt(km_hi, qh) + _dot_nt(km_lo, qh)
                gate = jnp.where(rows < i, gate, -jnp.inf)
                chosen = jnp.zeros((MOBA_TAGS, bs), jnp.float32)
                for _ in range(MOBA_TOPK):
                    best = jnp.max(gate, axis=0, keepdims=True)
                    first = jnp.min(jnp.where(gate == best, rows, MOBA_TAGS), axis=0, keepdims=True)
                    pick = (rows == first) & (best > -jnp.inf)
                    chosen = jnp.where(pick, 1.0, chosen)
                    gate = jnp.where(pick, -jnp.inf, gate)
                pad = jnp.zeros((hd, bs), jnp.float32)
                by_lane = jnp.concatenate([pad, chosen, jnp.zeros((LANES - hd - MOBA_TAGS, bs), jnp.float32)], axis=0).T
                sel.append(by_lane > 0.0)
            return sel

        def sb_bound(state, k):
            parts = []
            for t in range(n_tiles):
                both = functools.reduce(jnp.maximum, [state[t * per_tile + u][0] for u in range(per_tile)])
                parts.append(jnp.where(sb_starts[t] - k * step > 0, both, NEG_BIG))
            return jnp.max(functools.reduce(jnp.maximum, parts))

        def finish(state):
            out_t = jnp.concatenate([acc / l for (l, acc) in state], axis=0)
            return out_t.T

        def walk(q_past, step_fn, state, pair_groups):
            def scores(blk):
                start = pl.multiple_of(blk * bs, bs)
                return tuple(_dot_nt(kaug_ref[h, pl.ds(start, bs), :], q_past[h]) for h in heads)

            def visit(first_blk, n_blk, state):
                state = list(state)
                s_next = scores(first_blk) if n_blk else None
                for c in range(n_blk):
                    s_cur, s_next = s_next, (scores(first_blk + c + 1) if c + 1 < n_blk else None)
                    for h in heads:
                        state[h] = step_fn(state[h], s_cur[h], vt_ref[first_blk + c, h * hd:(h + 1) * hd, :])
                return tuple(state)

            if not pair_groups:
                return lax.fori_loop(0, (i + grp - 1) // grp, lambda g, st: visit(g * grp, grp, st), tuple(state))
            whole = i // grp
            state = lax.fori_loop(0, whole // 2, lambda g, st: visit(g * 2 * grp, 2 * grp, st), tuple(state))
            state = lax.fori_loop(whole // 2 * 2, whole, lambda g, st: visit(g * grp, grp, st), state)
            rest = [functools.partial(visit, whole * grp, r) for r in range(grp)]
            return lax.switch(i % grp, rest, state)

        def shifted():
            sel = select()
            q_past = [_bf16(jnp.where(lane < hd, dims[h], jnp.where(sel[h], -shift[h], NEG_BIG))) for h in heads]
            own = [_dot_nt(kaug_ref[h, pl.ds(own_start, bs), :],
                           _bf16(jnp.where(lane < hd, dims[h], jnp.where(lane == hd + i, -shift[h], 0.0))))
                   for h in heads]
            state = []
            for h in heads:
                p = jnp.where(causal, jnp.exp(own[h]), 0.0)
                state.append((jnp.sum(p, axis=0, keepdims=True), _dot(vt_own[h * hd:(h + 1) * hd, :], _bf16(p))))
            sb_state = tuple(_sb_weights(vs_ref, sb_jobs, sb_logs))
            sb_worst = sb_bound(sb_state, 0)

            def step_fn(st, s, vt):
                p = jnp.exp(s)
                return st[0] + jnp.sum(p, axis=0, keepdims=True), st[1] + _dot(vt, _bf16(p))

            return finish(walk(q_past, step_fn, state, True)), sb_state, sb_worst

        def running_max():
            sb_state = tuple(_sb_weights(vs_ref, sb_jobs, sb_logs))
            sb_worst = sb_bound(sb_state, 0)
            sel = select()
            q_past, state = [], []
            for h in heads:
                q_past.append(_bf16(jnp.where(lane < hd, dims[h], jnp.where(sel[h], 0.0, NEG_BIG))))
                q_own = _bf16(jnp.where(lane < hd, dims[h], 0.0))
                s = jnp.where(causal, _dot_nt(kaug_ref[h, pl.ds(own_start, bs), :], q_own), NEG_BIG)
                m = jnp.max(s, axis=0, keepdims=True)
                p = jnp.exp(s - m)
                state.append((m, jnp.sum(p, axis=0, keepdims=True), _dot(vt_own[h * hd:(h + 1) * hd, :], _bf16(p))))

            def step_fn(st, s, vt):
                m_old, l_old, acc_old = st
                m_new = jnp.maximum(m_old, jnp.max(s, axis=0, keepdims=True))
                p = jnp.exp(s - m_new)
                alpha = jnp.exp(m_old - m_new)
                return m_new, alpha * l_old + jnp.sum(p, axis=0, keepdims=True), alpha * acc_old + _dot(vt, _bf16(p))

            return finish([st[1:] for st in walk(q_past, step_fn, state, False)]), sb_state, sb_worst

        largest = jnp.max(functools.reduce(jnp.maximum, shift))
        sb_logs = _sb_logs(tri_ref, sb_jobs, _sb_scores(ks_ref, sb_jobs))
        moba_out, sb_state, sb_worst = lax.cond(2.0 * largest <= MOBA_SHIFT_MAX, shifted, running_max)
        o_ref[r0:r0 + bs, :] = _bf16(moba_out)

        def cond(loop):
            return loop[1] > SB_UNDERFLOW

        def body(loop):
            k, _, state = loop
            jobs = []
            for t in range(n_tiles):
                left = sb_starts[t] - k * step
                start = pl.multiple_of(jnp.maximum(left - step, 0), step)
                live = jnp.broadcast_to(left > 0, (tq, step))
                for u in range(per_tile):
                    job = t * per_tile + u
                    jobs.append((sb_q[job], sb_jobs[job][1], start, step, state[job], live))
            state = tuple(_sb_weights(vs_ref, jobs, _sb_logs(tri_ref, jobs, _sb_scores(ks_ref, jobs))))
            return k + 1, sb_bound(state, k + 1), state

        _, _, sb_state = lax.while_loop(cond, body, (0, sb_worst, sb_state))
        for t in range(n_tiles):
            for pr in range(ATTN_PAIRS):
                job = t * per_tile + pr * HEADS_PER_TILE
                os_ref[r0 + t * tq:r0 + (t + 1) * tq, pr * LANES:(pr + 1) * LANES] = _bf16(
                    jnp.where(sb_lane < hd, sb_state[job][1], sb_state[job + 1][1]))

    for qb in range(ATTN_QBLOCKS):
        block(step_id * ATTN_QBLOCKS + qb, qb * bs)


def _attention(qkv, batch, seq):
    t = qkv.shape[0]
    n_blocks = seq // MOBA_BLOCK
    assert n_blocks % MOBA_GROUP == 0, "the grouped block walk reads whole groups"
    assert n_blocks <= MOBA_TAGS, "one block tag lane per key block"
    assert MOBA_BLOCK % SB_TQ == 0 and seq >= SB_WINDOW and SB_WIDTH == MOBA_WIDTH
    assert n_blocks % ATTN_QBLOCKS == 0
    nq = n_blocks // ATTN_QBLOCKS
    step_rows = ATTN_QBLOCKS * MOBA_BLOCK
    width = ATTN_PAIRS * LANES
    hp = MOBA_WIDTH // width
    n_heads = ATTN_PAIRS * HEADS_PER_TILE
    idx = jnp.arange(SB_WINDOW)
    tri = _bf16(idx[:, None] > idx[None, :])
    tri2 = jnp.concatenate([tri, tri], axis=0)
    rows_blk = lambda sec: pl.BlockSpec((step_rows, width), lambda b, p, i: (b * nq + i, sec * hp + p))
    seq_blk = lambda sec, bufs: pl.BlockSpec((seq, width), lambda b, p, i: (b, sec * hp + p),
                                             pipeline_mode=pl.Buffered(bufs))
    out_blk = pl.BlockSpec((step_rows, width), lambda b, p, i: (b * nq + i, p))
    return pl.pallas_call(
        functools.partial(_attn_kernel, n_blocks=n_blocks),
        grid=(batch, hp, nq),
        in_specs=[rows_blk(0), seq_blk(1, 2), seq_blk(2, 2),
                  pl.BlockSpec((2 * SB_WINDOW, SB_WINDOW), lambda b, p, i: (0, 0)),
                  rows_blk(3), seq_blk(4, 1), seq_blk(5, 1)],
        out_specs=[out_blk, out_blk],
        out_shape=[jax.ShapeDtypeStruct((t, SB_WIDTH), jnp.bfloat16),
                   jax.ShapeDtypeStruct((t, MOBA_WIDTH), jnp.bfloat16)],
        scratch_shapes=[pltpu.VMEM((ATTN_PAIRS, LANES, LANES), jnp.float32),
                        pltpu.VMEM((n_heads, 1, LANES), jnp.float32),
                        pltpu.VMEM((n_heads, seq, LANES), jnp.bfloat16),
                        pltpu.VMEM((n_blocks, width, MOBA_BLOCK), jnp.bfloat16)],
        compiler_params=_params(3),
        name="attention",
    )(qkv, qkv, qkv, tri2, qkv, qkv, qkv)


def _mix_kernel(x_ref, osb_ref, omb_ref, wgs_ref, wgm_ref, wbs_ref, wbm_ref, wo_ref, g_ref, b_ref, o_ref, *, alpha):
    rows = x_ref.shape[0] // MIX_CHUNKS
    for c in range(MIX_CHUNKS):
        r = slice(c * rows, (c + 1) * rows)
        x = x_ref[r, :]
        xb = _bf16(x)
        branch_sb = _dot(osb_ref[r, :], wbs_ref[...])
        branch_mb = _dot(omb_ref[r, :], wbm_ref[...])
        merged = (jax.nn.sigmoid(_dot(xb, wgs_ref[...])) * branch_sb
                  + jax.nn.sigmoid(_dot(xb, wgm_ref[...])) * branch_mb)
        mix = _dot(_bf16(merged), wo_ref[...])
        o_ref[r, :] = _layer_norm(alpha * x + mix, g_ref[...], b_ref[...])


def _mix_block(x2, o_sb, o_mb, w_in, w_bsb, w_bmb, w_out, ln_g, ln_b, l, alpha, tm):
    t, d = x2.shape
    assert QKV_COLS % d == 0
    rows = lambda w: pl.BlockSpec((tm, w), lambda i: (i, 0))
    whole = lambda a: _layer_block(l, a.shape[1:])
    gate_cols = lambda j: _layer_block(l, (d, d), QKV_COLS // d + j)
    return pl.pallas_call(
        functools.partial(_mix_kernel, alpha=alpha),
        grid=(t // tm,),
        in_specs=[rows(d), rows(SB_WIDTH), rows(MOBA_WIDTH),
                  gate_cols(0), gate_cols(1), whole(w_bsb), whole(w_bmb), whole(w_out), whole(ln_g), whole(ln_b)],
        out_specs=rows(d),
        out_shape=jax.ShapeDtypeStruct((t, d), jnp.float32),
        compiler_params=_params(1),
        name="mix_ln",
    )(x2, o_sb, o_mb, w_in, w_in, w_bsb, w_bmb, w_out, ln_g, ln_b)


def _ffn_kernel(x_ref, wg_ref, wu_ref, wd_ref, g_ref, b_ref, o_ref, *, alpha, chunk):
    x = x_ref[...]
    xb = _bf16(x)
    acc = alpha * x
    for c in range(0, wg_ref.shape[1], chunk):
        gate = _dot(xb, wg_ref[:, c:c + chunk])
        up = _dot(xb, wu_ref[:, c:c + chunk])
        acc = acc + _dot(_bf16(jax.nn.silu(gate) * up), wd_ref[c:c + chunk, :])
    o_ref[...] = _layer_norm(acc, g_ref[...], b_ref[...])


def _ffn_chunk(d_ff):
    for c in range(MXU_WIDTH, d_ff, MXU_WIDTH):
        if d_ff % c == 0:
            return c
    return d_ff


def _ffn_block(x2, w_gate, w_up, w_down, ln_g, ln_b, l, alpha, tm):
    t, d = x2.shape
    rows = pl.BlockSpec((tm, d), lambda i: (i, 0))
    whole = lambda a: _layer_block(l, a.shape[1:])
    return pl.pallas_call(
        functools.partial(_ffn_kernel, alpha=alpha, chunk=_ffn_chunk(w_gate.shape[2])),
        grid=(t // tm,),
        in_specs=[rows, whole(w_gate), whole(w_up), whole(w_down), whole(ln_g), whole(ln_b)],
        out_specs=rows,
        out_shape=jax.ShapeDtypeStruct((t, d), jnp.float32),
        compiler_params=_params(1),
        name="ffn_ln",
    )(x2, w_gate, w_up, w_down, ln_g, ln_b)


def _rope_tables(seq):
    half = ROPE_DIM // 2
    inv_freq = ROPE_THETA ** (-jnp.arange(0, ROPE_DIM, 2, dtype=jnp.float32) / ROPE_DIM)
    dim = np.arange(LANES) % HEAD_DIM
    ang = jnp.arange(seq).astype(jnp.float32)[:, None] * inv_freq[dim % half][None, :]
    cos, sin = jnp.cos(ang), jnp.sin(ang)
    rotary = dim < ROPE_DIM
    return (jnp.where(rotary, cos, 1.0),
            jnp.where(rotary & (dim >= half), sin, 0.0),
            jnp.where(dim < half, -sin, 0.0))


def kernel(x, w_in, w_branch_sb, w_branch_moba, w_out, ln_mix_g, ln_mix_b,
           w_ffn_gate, w_ffn_up, w_ffn_down, ln_ffn_g, ln_ffn_b):
    batch, seq, d = x.shape
    depth = w_in.shape[0]
    assert w_in.shape[2] == QKV_COLS + 2 * d
    assert seq % MOBA_BLOCK == 0
    alpha = (2 * depth) ** 0.25
    tm = min(1024, seq)
    cos_t, sina_t, sinb_t = _rope_tables(seq)
    x2 = x.reshape(batch * seq, d)
    w_in, w_bsb, w_bmb, w_o = _bf16(w_in), _bf16(w_branch_sb), _bf16(w_branch_moba), _bf16(w_out)
    w_fg, w_fu, w_fd = _bf16(w_ffn_gate), _bf16(w_ffn_up), _bf16(w_ffn_down)
    row = lambda a: a[:, None, :]
    for l in range(depth):
        qkv = _qkv_proj(x2, w_in, l, cos_t, sina_t, sinb_t, seq, tm)
        o_sb, o_mb = _attention(qkv, batch, seq)
        x2 = _mix_block(x2, o_sb, o_mb, w_in, w_bsb, w_bmb, w_o, row(ln_mix_g), row(ln_mix_b), l, alpha, tm)
        x2 = _ffn_block(x2, w_fg, w_fu, w_fd, row(ln_ffn_g), row(ln_ffn_b), l, alpha, tm)
    return x2.reshape(batch, seq, d)
```

```python
import functools

import jax
import jax.numpy as jnp
import numpy as np
from jax import lax
from jax.experimental import pallas as pl
from jax.experimental.pallas import tpu as pltpu

HEAD_DIM = 64
SB_HEADS = 8
MOBA_HEADS = 8
SB_WIDTH = SB_HEADS * HEAD_DIM
MOBA_WIDTH = MOBA_HEADS * HEAD_DIM
QKV_COLS = 3 * SB_WIDTH + 3 * MOBA_WIDTH
MOBA_BLOCK = 256
MOBA_TOPK = 3
ROPE_THETA = 500000.0
ROPE_DIM = HEAD_DIM // 4
LN_EPS = 1e-5
SCALE = HEAD_DIM ** -0.5

LANES = 128
HEADS_PER_TILE = LANES // HEAD_DIM
VMEM_LIMIT = 56 * 1024 * 1024
MXU_WIDTH = 256
MIX_CHUNKS = 4
PROJ_TN = 512
SB_TQ = 64
SB_WINDOW = 256
SB_STEP = 64
MOBA_GROUP = 4
ATTN_PAIRS = 2
ATTN_QBLOCKS = 4
MOBA_TAGS = 32
MOBA_SHIFT_MAX = 60.0
MOBA_NORM_SLACK = 1.01
NEG_BIG = -1e30
SB_UNDERFLOW = -104.0

_NT = (((1,), (1,)), ((), ()))


def _bf16(a):
    return a.astype(jnp.bfloat16)


def _dot(a, b):
    return jnp.dot(a, b, preferred_element_type=jnp.float32)


def _dot_nt(a, b):
    return lax.dot_general(a, b, _NT, preferred_element_type=jnp.float32)


def _split_dot(a_f32, tri2_ref, width):
    hi = _bf16(a_f32)
    lo = _bf16(a_f32 - hi.astype(jnp.float32))
    if width == SB_WINDOW:
        return _dot(jnp.concatenate([hi, lo], axis=1), tri2_ref[...])
    tri = tri2_ref[:width, :width]
    return _dot(jnp.concatenate([hi, lo], axis=1), jnp.concatenate([tri, tri], axis=0))


def _layer_norm(y, g, b):
    mu = jnp.mean(y, axis=-1, keepdims=True)
    d = y - mu
    var = jnp.mean(d * d, axis=-1, keepdims=True)
    return d * lax.rsqrt(var + LN_EPS) * g + b


def _params(n_grid):
    return pltpu.CompilerParams(dimension_semantics=("arbitrary",) * n_grid,
                                vmem_limit_bytes=VMEM_LIMIT)


def _qkv_kernel(x_ref, w_ref, cos_ref, sina_ref, sinb_ref, o_ref, *, rope_lo, rope_hi):
    xb = _bf16(x_ref[...])
    reps = PROJ_TN // LANES
    wide = lambda r: jnp.concatenate([r[...]] * reps, axis=1)
    for j in range(w_ref.shape[1] // PROJ_TN):
        cols = slice(j * PROJ_TN, (j + 1) * PROJ_TN)
        acc = _dot(xb, w_ref[:, cols])
        if rope_lo <= j < rope_hi:
            acc = (acc * wide(cos_ref)
                   + pltpu.roll(acc, ROPE_DIM // 2, 1) * wide(sina_ref)
                   + pltpu.roll(acc, PROJ_TN - ROPE_DIM // 2, 1) * wide(sinb_ref))
        o_ref[:, cols] = _bf16(acc)


def _layer_block(l, shape, col=0):
    return pl.BlockSpec((None,) + tuple(shape), lambda i: (l, 0, col), pipeline_mode=pl.Buffered(1))


def _qkv_proj(x2, w_in, l, cos_t, sina_t, sinb_t, seq, tm):
    t, d = x2.shape
    n = QKV_COLS
    s_tiles = seq // tm
    tab = pl.BlockSpec((tm, LANES), lambda i: (i % s_tiles, 0))
    rope_lo = 3 * SB_WIDTH // PROJ_TN
    rope_hi = (3 * SB_WIDTH + 2 * MOBA_WIDTH) // PROJ_TN
    return pl.pallas_call(
        functools.partial(_qkv_kernel, rope_lo=rope_lo, rope_hi=rope_hi),
        grid=(t // tm,),
        in_specs=[pl.BlockSpec((tm, d), lambda i: (i, 0)),
                  _layer_block(l, (d, n)), tab, tab, tab],
        out_specs=pl.BlockSpec((tm, n), lambda i: (i, 0)),
        out_shape=jax.ShapeDtypeStruct((t, n), jnp.bfloat16),
        compiler_params=_params(1),
        name="qkv_proj",
    )(x2, w_in, cos_t, sina_t, sinb_t)


def _sb_scores(k_ref, jobs):
    return [_dot_nt(q, k_ref[pl.ds(start, width), tile * LANES:(tile + 1) * LANES])
            for (q, tile, start, width, _, _) in jobs]


def _sb_logs(tri_ref, jobs, zs):
    log_keep, log_beta = [], []
    for (_, _, _, _, _, mask), z in zip(jobs, zs):
        lb = jnp.minimum(z, 0.0) - jnp.log(1.0 + jnp.exp(-jnp.abs(z)))
        lk = lb - z
        log_keep.append(lk if mask is None else jnp.where(mask, lk, 0.0))
        log_beta.append(lb)
    suffix = [_split_dot(lk, tri_ref, job[3]) for job, lk in zip(jobs, log_keep)]
    return log_keep, log_beta, suffix


def _sb_weights(v_ref, jobs, logs):
    log_keep, log_beta, suffix = logs
    out = []
    for u, (_, tile, start, width, (carry, acc), mask) in enumerate(jobs):
        w = jnp.exp(log_beta[u] + suffix[u] + carry)
        w = _bf16(w if mask is None else jnp.where(mask, w, 0.0))
        out.append((carry + suffix[u][:, :1] + log_keep[u][:, :1],
                    acc + _dot(w, v_ref[pl.ds(start, width), tile * LANES:(tile + 1) * LANES])))
    return out


def _attn_kernel(qs_ref, ks_ref, vs_ref, tri_ref, q_ref, k_ref, v_ref, os_ref, o_ref,
                 kmean_ref, knorm_ref, kaug_ref, vt_ref, *, n_blocks):
    step_id = pl.program_id(2)
    bs = MOBA_BLOCK
    hd = HEAD_DIM
    grp = MOBA_GROUP
    heads = range(ATTN_PAIRS * HEADS_PER_TILE)
    half_of = lambda h: h % HEADS_PER_TILE
    lane = lax.broadcasted_iota(jnp.int32, (bs, LANES), 1)

    @pl.when(step_id == 0)
    def _():
        kmean_ref[...] = jnp.zeros_like(kmean_ref)
        r_i = lax.broadcasted_iota(jnp.int32, (8 * HEADS_PER_TILE, LANES), 0)
        l_i = lax.broadcasted_iota(jnp.int32, (8 * HEADS_PER_TILE, LANES), 1)
        head_lanes = _bf16(r_i // 8 == l_i // hd)
        norm2 = [jnp.zeros((8 * HEADS_PER_TILE, bs), jnp.float32)] * ATTN_PAIRS
        for n in range(n_blocks):
            kbs = [k_ref[n * bs:(n + 1) * bs, pr * LANES:(pr + 1) * LANES].astype(jnp.float32)
                   for pr in range(ATTN_PAIRS)]
            for pr in range(ATTN_PAIRS):
                kmean_ref[pr, hd + n:hd + n + 1, :] = jnp.sum(kbs[pr], axis=0, keepdims=True) * (1.0 / bs)
                norm2[pr] = jnp.maximum(norm2[pr], _dot_nt(head_lanes, _bf16(kbs[pr] * kbs[pr])))
            tag = (lane == hd + n).astype(jnp.float32)
            for h in heads:
                kb = kbs[h // HEADS_PER_TILE]
                dims = kb if half_of(h) == 0 else pltpu.roll(kb, LANES - half_of(h) * hd, 1)
                kaug_ref[h, n * bs:(n + 1) * bs, :] = _bf16(jnp.where(lane < hd, dims, tag))
            vt_ref[n] = _bf16(v_ref[n * bs:(n + 1) * bs, :].astype(jnp.float32).T)
        for h in heads:
            per_key = norm2[h // HEADS_PER_TILE][8 * half_of(h):8 * half_of(h) + 1]
            knorm_ref[h] = jnp.broadcast_to(jnp.sqrt(jnp.max(per_key, axis=1, keepdims=True)), (1, LANES))

    def block(i, r0):
        tq, win, step = SB_TQ, SB_WINDOW, SB_STEP
        n_tiles = bs // tq
        sb_lane = lax.broadcasted_iota(jnp.int32, (tq, LANES), 1)
        sb_row = lax.broadcasted_iota(jnp.int32, (tq, win), 0)
        sb_col = lax.broadcasted_iota(jnp.int32, (tq, win), 1)
        zero = (jnp.zeros((tq, 1), jnp.float32), jnp.zeros((tq, LANES), jnp.float32))
        sb_jobs, sb_starts, sb_q = [], [], []
        for t in range(n_tiles):
            first_q = i * bs + t * tq
            start0 = pl.multiple_of(jnp.maximum(first_q + tq - win, 0), step)
            past = sb_col - sb_row < first_q - start0
            sb_starts.append(start0)
            for pr in range(ATTN_PAIRS):
                q = qs_ref[r0 + t * tq:r0 + (t + 1) * tq, pr * LANES:(pr + 1) * LANES]
                for h in range(HEADS_PER_TILE):
                    qh = jnp.where(sb_lane // hd == h, q, jnp.zeros_like(q)) * SCALE
                    sb_q.append(qh)
                    sb_jobs.append((qh, pr, start0, win, zero, past))
        per_tile = ATTN_PAIRS * HEADS_PER_TILE

        qs_in = [q_ref[r0:r0 + bs, pr * LANES:(pr + 1) * LANES] for pr in range(ATTN_PAIRS)]
        qfs = [qq.astype(jnp.float32) for qq in qs_in]
        key = lax.broadcasted_iota(jnp.int32, (bs, bs), 0)
        qry = lax.broadcasted_iota(jnp.int32, (bs, bs), 1)
        rows = lax.broadcasted_iota(jnp.int32, (MOBA_TAGS, bs), 0)
        causal = key <= qry
        kmeans = [kmean_ref[pr, hd:hd + MOBA_TAGS, :] for pr in range(ATTN_PAIRS)]
        km_his = [_bf16(km) for km in kmeans]
        km_los = [_bf16(km - hi.astype(jnp.float32)) for km, hi in zip(kmeans, km_his)]
        own_start = pl.multiple_of(i * bs, bs)
        vt_own = vt_ref[i]

        dims, shift = [], []
        for h in heads:
            qf = qfs[h // HEADS_PER_TILE]
            d = (qf if half_of(h) == 0 else pltpu.roll(qf, LANES - half_of(h) * hd, 1)) * SCALE
            dims.append(d)
            norm = jnp.sqrt(jnp.sum(jnp.where(lane < hd, d * d, 0.0), axis=1, keepdims=True))
            shift.append(norm * knorm_ref[h] * MOBA_NORM_SLACK)

        def select():
            sel = []
            for h in heads:
                q = qs_in[h // HEADS_PER_TILE]
                km_hi, km_lo = km_his[h // HEADS_PER_TILE], km_los[h // HEADS_PER_TILE]
                qh = jnp.where(lane // hd == half_of(h), q, jnp.zeros_like(q))
                gate = _dot_nt(km_hi, qh) + _dot_nt(km_lo, qh)
                gate = jnp.where(rows < i, gate, -jnp.inf)
                chosen = jnp.zeros((MOBA_TAGS, bs), jnp.float32)
                for _ in range(MOBA_TOPK):
                    best = jnp.max(gate, axis=0, keepdims=True)
                    first = jnp.min(jnp.where(gate == best, rows, MOBA_TAGS), axis=0, keepdims=True)
                    pick = (rows == first) & (best > -jnp.inf)
                    chosen = jnp.where(pick, 1.0, chosen)
                    gate = jnp.where(pick, -jnp.inf, gate)
                pad = jnp.zeros((hd, bs), jnp.float32)
                by_lane = jnp.concatenate([pad, chosen, jnp.zeros((LANES - hd - MOBA_TAGS, bs), jnp.float32)], axis=0).T
                sel.append(by_lane > 0.0)
            return sel

        def sb_bound(state, k):
            parts = []
            for t in range(n_tiles):
                both = functools.reduce(jnp.maximum, [state[t * per_tile + u][0] for u in range(per_tile)])
                parts.append(jnp.where(sb_starts[t] - k * step > 0, both, NEG_BIG))
            return jnp.max(functools.reduce(jnp.maximum, parts))

        def finish(state):
            out_t = jnp.concatenate([acc / l for (l, acc) in state], axis=0)
            return out_t.T

        def walk(q_past, step_fn, state, pair_groups):
            def scores(blk):
                start = pl.multiple_of(blk * bs, bs)
                return tuple(_dot_nt(kaug_ref[h, pl.ds(start, bs), :], q_past[h]) for h in heads)

            def visit(first_blk, n_blk, state):
                state = list(state)
                s_next = scores(first_blk) if n_blk else None
                for c in range(n_blk):
                    s_cur, s_next = s_next, (scores(first_blk + c + 1) if c + 1 < n_blk else None)
                    for h in heads:
                        state[h] = step_fn(state[h], s_cur[h], vt_ref[first_blk + c, h * hd:(h + 1) * hd, :])
                return tuple(state)

            if not pair_groups:
                return lax.fori_loop(0, (i + grp - 1) // grp, lambda g, st: visit(g * grp, grp, st), tuple(state))
            whole = i // grp
            state = lax.fori_loop(0, whole // 2, lambda g, st: visit(g * 2 * grp, 2 * grp, st), tuple(state))
            state = lax.fori_loop(whole // 2 * 2, whole, lambda g, st: visit(g * grp, grp, st), state)
            rest = [functools.partial(visit, whole * grp, r) for r in range(grp)]
            return lax.switch(i % grp, rest, state)

        def shifted():
            sel = select()
            q_past = [_bf16(jnp.where(lane < hd, dims[h], jnp.where(sel[h], -shift[h], NEG_BIG))) for h in heads]
            own = [_dot_nt(kaug_ref[h, pl.ds(own_start, bs), :],
                           _bf16(jnp.where(lane < hd, dims[h], jnp.where(lane == hd + i, -shift[h], 0.0))))
                   for h in heads]
            state = []
            for h in heads:
                p = jnp.where(causal, jnp.exp(own[h]), 0.0)
                state.append((jnp.sum(p, axis=0, keepdims=True), _dot(vt_own[h * hd:(h + 1) * hd, :], _bf16(p))))
            sb_state = tuple(_sb_weights(vs_ref, sb_jobs, sb_logs))
            sb_worst = sb_bound(sb_state, 0)

            def step_fn(st, s, vt):
                p = jnp.exp(s)
                return st[0] + jnp.sum(p, axis=0, keepdims=True), st[1] + _dot(vt, _bf16(p))

            return finish(walk(q_past, step_fn, state, True)), sb_state, sb_worst

        def running_max():
            sb_state = tuple(_sb_weights(vs_ref, sb_jobs, sb_logs))
            sb_worst = sb_bound(sb_state, 0)
            sel = select()
            q_past, state = [], []
            for h in heads:
                q_past.append(_bf16(jnp.where(lane < hd, dims[h], jnp.where(sel[h], 0.0, NEG_BIG))))
                q_own = _bf16(jnp.where(lane < hd, dims[h], 0.0))
                s = jnp.where(causal, _dot_nt(kaug_ref[h, pl.ds(own_start, bs), :], q_own), NEG_BIG)
                m = jnp.max(s, axis=0, keepdims=True)
                p = jnp.exp(s - m)
                state.append((m, jnp.sum(p, axis=0, keepdims=True), _dot(vt_own[h * hd:(h + 1) * hd, :], _bf16(p))))

            def step_fn(st, s, vt):
                m_old, l_old, acc_old = st
                m_new = jnp.maximum(m_old, jnp.max(s, axis=0, keepdims=True))
                p = jnp.exp(s - m_new)
                alpha = jnp.exp(m_old - m_new)
                return m_new, alpha * l_old + jnp.sum(p, axis=0, keepdims=True), alpha * acc_old + _dot(vt, _bf16(p))

            return finish([st[1:] for st in walk(q_past, step_fn, state, False)]), sb_state, sb_worst

        largest = jnp.max(functools.reduce(jnp.maximum, shift))
        sb_logs = _sb_logs(tri_ref, sb_jobs, _sb_scores(ks_ref, sb_jobs))
        moba_out, sb_state, sb_worst = lax.cond(2.0 * largest <= MOBA_SHIFT_MAX, shifted, running_max)
        o_ref[r0:r0 + bs, :] = _bf16(moba_out)

        def cond(loop):
            return loop[1] > SB_UNDERFLOW

        def body(loop):
            k, _, state = loop
            jobs = []
            for t in range(n_tiles):
                left = sb_starts[t] - k * step
                start = pl.multiple_of(jnp.maximum(left - step, 0), step)
                live = jnp.broadcast_to(left > 0, (tq, step))
                for u in range(per_tile):
                    job = t * per_tile + u
                    jobs.append((sb_q[job], sb_jobs[job][1], start, step, state[job], live))
            state = tuple(_sb_weights(vs_ref, jobs, _sb_logs(tri_ref, jobs, _sb_scores(ks_ref, jobs))))
            return k + 1, sb_bound(state, k + 1), state

        _, _, sb_state = lax.while_loop(cond, body, (0, sb_worst, sb_state))
        for t in range(n_tiles):
            for pr in range(ATTN_PAIRS):
                job = t * per_tile + pr * HEADS_PER_TILE
                os_ref[r0 + t * tq:r0 + (t + 1) * tq, pr * LANES:(pr + 1) * LANES] = _bf16(
                    jnp.where(sb_lane < hd, sb_state[job][1], sb_state[job + 1][1]))

    for qb in range(ATTN_QBLOCKS):
        block(step_id * ATTN_QBLOCKS + qb, qb * bs)


def _attention(qkv, batch, seq):
    t = qkv.shape[0]
    n_blocks = seq // MOBA_BLOCK
    assert n_blocks % MOBA_GROUP == 0, "the grouped block walk reads whole groups"
    assert n_blocks <= MOBA_TAGS, "one block tag lane per key block"
    assert MOBA_BLOCK % SB_TQ == 0 and seq >= SB_WINDOW and SB_WIDTH == MOBA_WIDTH
    assert n_blocks % ATTN_QBLOCKS == 0
    nq = n_blocks // ATTN_QBLOCKS
    step_rows = ATTN_QBLOCKS * MOBA_BLOCK
    width = ATTN_PAIRS * LANES
    hp = MOBA_WIDTH // width
    n_heads = ATTN_PAIRS * HEADS_PER_TILE
    idx = jnp.arange(SB_WINDOW)
    tri = _bf16(idx[:, None] > idx[None, :])
    tri2 = jnp.concatenate([tri, tri], axis=0)
    rows_blk = lambda sec: pl.BlockSpec((step_rows, width), lambda b, p, i: (b * nq + i, sec * hp + p))
    seq_blk = lambda sec, bufs: pl.BlockSpec((seq, width), lambda b, p, i: (b, sec * hp + p),
                                             pipeline_mode=pl.Buffered(bufs))
    out_blk = pl.BlockSpec((step_rows, width), lambda b, p, i: (b * nq + i, p))
    return pl.pallas_call(
        functools.partial(_attn_kernel, n_blocks=n_blocks),
        grid=(batch, hp, nq),
        in_specs=[rows_blk(0), seq_blk(1, 2), seq_blk(2, 2),
                  pl.BlockSpec((2 * SB_WINDOW, SB_WINDOW), lambda b, p, i: (0, 0)),
                  rows_blk(3), seq_blk(4, 1), seq_blk(5, 1)],
        out_specs=[out_blk, out_blk],
        out_shape=[jax.ShapeDtypeStruct((t, SB_WIDTH), jnp.bfloat16),
                   jax.ShapeDtypeStruct((t, MOBA_WIDTH), jnp.bfloat16)],
        scratch_shapes=[pltpu.VMEM((ATTN_PAIRS, LANES, LANES), jnp.float32),
                        pltpu.VMEM((n_heads, 1, LANES), jnp.float32),
                        pltpu.VMEM((n_heads, seq, LANES), jnp.bfloat16),
                        pltpu.VMEM((n_blocks, width, MOBA_BLOCK), jnp.bfloat16)],
        compiler_params=_params(3),
        name="attention",
    )(qkv, qkv, qkv, tri2, qkv, qkv, qkv)


def _mix_kernel(x_ref, osb_ref, omb_ref, wgs_ref, wgm_ref, wbs_ref, wbm_ref, wo_ref, g_ref, b_ref, o_ref, *, alpha):
    rows = x_ref.shape[0] // MIX_CHUNKS
    for c in range(MIX_CHUNKS):
        r = slice(c * rows, (c + 1) * rows)
        x = x_ref[r, :]
        xb = _bf16(x)
        branch_sb = _dot(osb_ref[r, :], wbs_ref[...])
        branch_mb = _dot(omb_ref[r, :], wbm_ref[...])
        merged = (jax.nn.sigmoid(_dot(xb, wgs_ref[...])) * branch_sb
                  + jax.nn.sigmoid(_dot(xb, wgm_ref[...])) * branch_mb)
        mix = _dot(_bf16(merged), wo_ref[...])
        o_ref[r, :] = _layer_norm(alpha * x + mix, g_ref[...], b_ref[...])


def _mix_block(x2, o_sb, o_mb, w_in, w_bsb, w_bmb, w_out, ln_g, ln_b, l, alpha, tm):
    t, d = x2.shape
    assert QKV_COLS % d == 0
    rows = lambda w: pl.BlockSpec((tm, w), lambda i: (i, 0))
    whole = lambda a: _layer_block(l, a.shape[1:])
    gate_cols = lambda j: _layer_block(l, (d, d), QKV_COLS // d + j)
    return pl.pallas_call(
        functools.partial(_mix_kernel, alpha=alpha),
        grid=(t // tm,),
        in_specs=[rows(d), rows(SB_WIDTH), rows(MOBA_WIDTH),
                  gate_cols(0), gate_cols(1), whole(w_bsb), whole(w_bmb), whole(w_out), whole(ln_g), whole(ln_b)],
        out_specs=rows(d),
        out_shape=jax.ShapeDtypeStruct((t, d), jnp.float32),
        compiler_params=_params(1),
        name="mix_ln",
    )(x2, o_sb, o_mb, w_in, w_in, w_bsb, w_bmb, w_out, ln_g, ln_b)


def _ffn_kernel(x_ref, wg_ref, wu_ref, wd_ref, g_ref, b_ref, o_ref, *, alpha, chunk):
    x = x_ref[...]
    xb = _bf16(x)
    acc = alpha * x
    for c in range(0, wg_ref.shape[1], chunk):
        gate = _dot(xb, wg_ref[:, c:c + chunk])
        up = _dot(xb, wu_ref[:, c:c + chunk])
        acc = acc + _dot(_bf16(jax.nn.silu(gate) * up), wd_ref[c:c + chunk, :])
    o_ref[...] = _layer_norm(acc, g_ref[...], b_ref[...])


def _ffn_chunk(d_ff):
    for c in range(MXU_WIDTH, d_ff, MXU_WIDTH):
        if d_ff % c == 0:
            return c
    return d_ff


def _ffn_block(x2, w_gate, w_up, w_down, ln_g, ln_b, l, alpha, tm):
    t, d = x2.shape
    rows = pl.BlockSpec((tm, d), lambda i: (i, 0))
    whole = lambda a: _layer_block(l, a.shape[1:])
    return pl.pallas_call(
        functools.partial(_ffn_kernel, alpha=alpha, chunk=_ffn_chunk(w_gate.shape[2])),
        grid=(t // tm,),
        in_specs=[rows, whole(w_gate), whole(w_up), whole(w_down), whole(ln_g), whole(ln_b)],
        out_specs=rows,
        out_shape=jax.ShapeDtypeStruct((t, d), jnp.float32),
        compiler_params=_params(1),
        name="ffn_ln",
    )(x2, w_gate, w_up, w_down, ln_g, ln_b)


def _rope_tables(seq):
    half = ROPE_DIM // 2
    inv_freq = ROPE_THETA ** (-jnp.arange(0, ROPE_DIM, 2, dtype=jnp.float32) / ROPE_DIM)
    dim = np.arange(LANES) % HEAD_DIM
    ang = jnp.arange(seq).astype(jnp.float32)[:, None] * inv_freq[dim % half][None, :]
    cos, sin = jnp.cos(ang), jnp.sin(ang)
    rotary = dim < ROPE_DIM
    return (jnp.where(rotary, cos, 1.0),
            jnp.where(rotary & (dim >= half), sin, 0.0),
            jnp.where(dim < half, -sin, 0.0))


def kernel(x, w_in, w_branch_sb, w_branch_moba, w_out, ln_mix_g, ln_mix_b,
           w_ffn_gate, w_ffn_up, w_ffn_down, ln_ffn_g, ln_ffn_b):
    batch, seq, d = x.shape
    depth = w_in.shape[0]
    assert w_in.shape[2] == QKV_COLS + 2 * d
    assert seq % MOBA_BLOCK == 0
    alpha = (2 * depth) ** 0.25
    tm = min(1024, seq)
    cos_t, sina_t, sinb_t = _rope_tables(seq)
    x2 = x.reshape(batch * seq, d)
    w_in, w_bsb, w_bmb, w_o = _bf16(w_in), _bf16(w_branch_sb), _bf16(w_branch_moba), _bf16(w_out)
    w_fg, w_fu, w_fd = _bf16(w_ffn_gate), _bf16(w_ffn_up), _bf16(w_ffn_down)
    row = lambda a: a[:, None, :]
    for l in range(depth):
        qkv = _qkv_proj(x2, w_in, l, cos_t, sina_t, sinb_t, seq, tm)
        o_sb, o_mb = _attention(qkv, batch, seq)
        x2 = _mix_block(x2, o_sb, o_mb, w_in, w_bsb, w_bmb, w_o, row(ln_mix_g), row(ln_mix_b), l, alpha, tm)
        x2 = _ffn_block(x2, w_fg, w_fu, w_fd, row(ln_ffn_g), row(ln_ffn_b), l, alpha, tm)
    return x2.reshape(batch, seq, d)
```

```python
import functools

import jax
import jax.numpy as jnp
import numpy as np
from jax import lax
from jax.experimental import pallas as pl
from jax.experimental.pallas import tpu as pltpu

HEAD_DIM = 64
SB_HEADS = 8
MOBA_HEADS = 8
SB_WIDTH = SB_HEADS * HEAD_DIM
MOBA_WIDTH = MOBA_HEADS * HEAD_DIM
QKV_COLS = 3 * SB_WIDTH + 3 * MOBA_WIDTH
MOBA_BLOCK = 256
MOBA_TOPK = 3
ROPE_THETA = 500000.0
ROPE_DIM = HEAD_DIM // 4
LN_EPS = 1e-5
SCALE = HEAD_DIM ** -0.5

LANES = 128
HEADS_PER_TILE = LANES // HEAD_DIM
VMEM_LIMIT = 56 * 1024 * 1024
MXU_WIDTH = 256
MIX_CHUNKS = 4
PROJ_TN = 512
SB_TQ = 64
SB_WINDOW = 256
SB_STEP = 64
MOBA_GROUP = 4
ATTN_PAIRS = 2
ATTN_QBLOCKS = 2
MOBA_TAGS = 32
MOBA_SHIFT_MAX = 60.0
MOBA_NORM_SLACK = 1.01
NEG_BIG = -1e30
SB_UNDERFLOW = -104.0

_NT = (((1,), (1,)), ((), ()))


def _bf16(a):
    return a.astype(jnp.bfloat16)


def _dot(a, b):
    return jnp.dot(a, b, preferred_element_type=jnp.float32)


def _dot_nt(a, b):
    return lax.dot_general(a, b, _NT, preferred_element_type=jnp.float32)


def _split_dot(a_f32, tri2_ref, width):
    hi = _bf16(a_f32)
    lo = _bf16(a_f32 - hi.astype(jnp.float32))
    if width == SB_WINDOW:
        return _dot(jnp.concatenate([hi, lo], axis=1), tri2_ref[...])
    tri = tri2_ref[:width, :width]
    return _dot(jnp.concatenate([hi, lo], axis=1), jnp.concatenate([tri, tri], axis=0))


def _layer_norm(y, g, b):
    mu = jnp.mean(y, axis=-1, keepdims=True)
    d = y - mu
    var = jnp.mean(d * d, axis=-1, keepdims=True)
    return d * lax.rsqrt(var + LN_EPS) * g + b


def _params(n_grid):
    return pltpu.CompilerParams(dimension_semantics=("arbitrary",) * n_grid,
                                vmem_limit_bytes=VMEM_LIMIT)


def _qkv_kernel(x_ref, w_ref, cos_ref, sina_ref, sinb_ref, o_ref, *, rope_lo, rope_hi):
    xb = _bf16(x_ref[...])
    reps = PROJ_TN // LANES
    wide = lambda r: jnp.concatenate([r[...]] * reps, axis=1)
    for j in range(w_ref.shape[1] // PROJ_TN):
        cols = slice(j * PROJ_TN, (j + 1) * PROJ_TN)
        acc = _dot(xb, _bf16(w_ref[:, cols]))
        if rope_lo <= j < rope_hi:
            acc = (acc * wide(cos_ref)
                   + pltpu.roll(acc, ROPE_DIM // 2, 1) * wide(sina_ref)
                   + pltpu.roll(acc, PROJ_TN - ROPE_DIM // 2, 1) * wide(sinb_ref))
        o_ref[:, cols] = _bf16(acc)


def _layer_block(l, shape, col=0):
    return pl.BlockSpec((None,) + tuple(shape), lambda i: (l, 0, col), pipeline_mode=pl.Buffered(1))


def _qkv_proj(x2, w_in, l, cos_t, sina_t, sinb_t, seq, tm):
    t, d = x2.shape
    n = QKV_COLS
    s_tiles = seq // tm
    tab = pl.BlockSpec((tm, LANES), lambda i: (i % s_tiles, 0))
    rope_lo = 3 * SB_WIDTH // PROJ_TN
    rope_hi = (3 * SB_WIDTH + 2 * MOBA_WIDTH) // PROJ_TN
    return pl.pallas_call(
        functools.partial(_qkv_kernel, rope_lo=rope_lo, rope_hi=rope_hi),
        grid=(t // tm,),
        in_specs=[pl.BlockSpec((tm, d), lambda i: (i, 0)),
                  _layer_block(l, (d, n)), tab, tab, tab],
        out_specs=pl.BlockSpec((tm, n), lambda i: (i, 0)),
        out_shape=jax.ShapeDtypeStruct((t, n), jnp.bfloat16),
        compiler_params=_params(1),
        name="qkv_proj",
    )(x2, w_in, cos_t, sina_t, sinb_t)


def _sb_scores(k_ref, jobs):
    return [_dot_nt(q, k_ref[pl.ds(start, width), tile * LANES:(tile + 1) * LANES])
            for (q, tile, start, width, _, _) in jobs]


def _sb_logs(tri_ref, jobs, zs):
    log_keep, log_beta = [], []
    for (_, _, _, _, _, mask), z in zip(jobs, zs):
        lb = jnp.minimum(z, 0.0) - jnp.log(1.0 + jnp.exp(-jnp.abs(z)))
        lk = lb - z
        log_keep.append(lk if mask is None else jnp.where(mask, lk, 0.0))
        log_beta.append(lb)
    suffix = [_split_dot(lk, tri_ref, job[3]) for job, lk in zip(jobs, log_keep)]
    return log_keep, log_beta, suffix


def _sb_weights(v_ref, jobs, logs):
    log_keep, log_beta, suffix = logs
    out = []
    for u, (_, tile, start, width, (carry, acc), mask) in enumerate(jobs):
        w = jnp.exp(log_beta[u] + suffix[u] + carry)
        w = _bf16(w if mask is None else jnp.where(mask, w, 0.0))
        out.append((carry + suffix[u][:, :1] + log_keep[u][:, :1],
                    acc + _dot(w, v_ref[pl.ds(start, width), tile * LANES:(tile + 1) * LANES])))
    return out


def _attn_kernel(qs_ref, ks_ref, vs_ref, tri_ref, q_ref, k_ref, v_ref, os_ref, o_ref,
                 kmean_ref, knorm_ref, kaug_ref, vt_ref, *, n_blocks):
    step_id = pl.program_id(2)
    bs = MOBA_BLOCK
    hd = HEAD_DIM
    grp = MOBA_GROUP
    heads = range(ATTN_PAIRS * HEADS_PER_TILE)
    half_of = lambda h: h % HEADS_PER_TILE
    lane = lax.broadcasted_iota(jnp.int32, (bs, LANES), 1)

    @pl.when(step_id == 0)
    def _():
        kmean_ref[...] = jnp.zeros_like(kmean_ref)
        r_i = lax.broadcasted_iota(jnp.int32, (8 * HEADS_PER_TILE, LANES), 0)
        l_i = lax.broadcasted_iota(jnp.int32, (8 * HEADS_PER_TILE, LANES), 1)
        head_lanes = _bf16(r_i // 8 == l_i // hd)
        norm2 = [jnp.zeros((8 * HEADS_PER_TILE, bs), jnp.float32)] * ATTN_PAIRS
        for n in range(n_blocks):
            kbs = [k_ref[n * bs:(n + 1) * bs, pr * LANES:(pr + 1) * LANES].astype(jnp.float32)
                   for pr in range(ATTN_PAIRS)]
            for pr in range(ATTN_PAIRS):
                kmean_ref[pr, hd + n:hd + n + 1, :] = jnp.sum(kbs[pr], axis=0, keepdims=True) * (1.0 / bs)
                norm2[pr] = jnp.maximum(norm2[pr], _dot_nt(head_lanes, _bf16(kbs[pr] * kbs[pr])))
            tag = (lane == hd + n).astype(jnp.float32)
            for h in heads:
                kb = kbs[h // HEADS_PER_TILE]
                dims = kb if half_of(h) == 0 else pltpu.roll(kb, LANES - half_of(h) * hd, 1)
                kaug_ref[h, n * bs:(n + 1) * bs, :] = _bf16(jnp.where(lane < hd, dims, tag))
            vt_ref[n] = _bf16(v_ref[n * bs:(n + 1) * bs, :].astype(jnp.float32).T)
        for h in heads:
            per_key = norm2[h // HEADS_PER_TILE][8 * half_of(h):8 * half_of(h) + 1]
            knorm_ref[h] = jnp.broadcast_to(jnp.sqrt(jnp.max(per_key, axis=1, keepdims=True)), (1, LANES))

    def block(i, r0):
        tq, win, step = SB_TQ, SB_WINDOW, SB_STEP
        n_tiles = bs // tq
        sb_lane = lax.broadcasted_iota(jnp.int32, (tq, LANES), 1)
        sb_row = lax.broadcasted_iota(jnp.int32, (tq, win), 0)
        sb_col = lax.broadcasted_iota(jnp.int32, (tq, win), 1)
        zero = (jnp.zeros((tq, 1), jnp.float32), jnp.zeros((tq, LANES), jnp.float32))
        sb_jobs, sb_starts, sb_q = [], [], []
        for t in range(n_tiles):
            first_q = i * bs + t * tq
            start0 = pl.multiple_of(jnp.maximum(first_q + tq - win, 0), step)
            past = sb_col - sb_row < first_q - start0
            sb_starts.append(start0)
            for pr in range(ATTN_PAIRS):
                q = qs_ref[r0 + t * tq:r0 + (t + 1) * tq, pr * LANES:(pr + 1) * LANES]
                for h in range(HEADS_PER_TILE):
                    qh = jnp.where(sb_lane // hd == h, q, jnp.zeros_like(q)) * SCALE
                    sb_q.append(qh)
                    sb_jobs.append((qh, pr, start0, win, zero, past))
        per_tile = ATTN_PAIRS * HEADS_PER_TILE

        qs_in = [q_ref[r0:r0 + bs, pr * LANES:(pr + 1) * LANES] for pr in range(ATTN_PAIRS)]
        qfs = [qq.astype(jnp.float32) for qq in qs_in]
        key = lax.broadcasted_iota(jnp.int32, (bs, bs), 0)
        qry = lax.broadcasted_iota(jnp.int32, (bs, bs), 1)
        rows = lax.broadcasted_iota(jnp.int32, (MOBA_TAGS, bs), 0)
        causal = key <= qry
        kmeans = [kmean_ref[pr, hd:hd + MOBA_TAGS, :] for pr in range(ATTN_PAIRS)]
        km_his = [_bf16(km) for km in kmeans]
        km_los = [_bf16(km - hi.astype(jnp.float32)) for km, hi in zip(kmeans, km_his)]
        own_start = pl.multiple_of(i * bs, bs)
        vt_own = vt_ref[i]

        dims, shift = [], []
        for h in heads:
            qf = qfs[h // HEADS_PER_TILE]
            d = (qf if half_of(h) == 0 else pltpu.roll(qf, LANES - half_of(h) * hd, 1)) * SCALE
            dims.append(d)
            norm = jnp.sqrt(jnp.sum(jnp.where(lane < hd, d * d, 0.0), axis=1, keepdims=True))
            shift.append(norm * knorm_ref[h] * MOBA_NORM_SLACK)

        def select():
            sel = []
            for h in heads:
                q = qs_in[h // HEADS_PER_TILE]
                km_hi, km_lo = km_his[h // HEADS_PER_TILE], km_los[h // HEADS_PER_TILE]
                qh = jnp.where(lane // hd == half_of(h), q, jnp.zeros_like(q))
                gate = _dot_nt(km_hi, qh) + _dot_nt(km_lo, qh)
                gate = jnp.where(rows < i, gate, -jnp.inf)
                chosen = jnp.zeros((MOBA_TAGS, bs), jnp.float32)
                for _ in range(MOBA_TOPK):
                    best = jnp.max(gate, axis=0, keepdims=True)
                    first = jnp.min(jnp.where(gate == best, rows, MOBA_TAGS), axis=0, keepdims=True)
                    pick = (rows == first) & (best > -jnp.inf)
                    chosen = jnp.where(pick, 1.0, chosen)
                    gate = jnp.where(pick, -jnp.inf, gate)
                pad = jnp.zeros((hd, bs), jnp.float32)
                by_lane = jnp.concatenate([pad, chosen, jnp.zeros((LANES - hd - MOBA_TAGS, bs), jnp.float32)], axis=0).T
                sel.append(by_lane > 0.0)
            return sel

        def sb_bound(state, k):
            parts = []
            for t in range(n_tiles):
                both = functools.reduce(jnp.maximum, [state[t * per_tile + u][0] for u in range(per_tile)])
                parts.append(jnp.where(sb_starts[t] - k * step > 0, both, NEG_BIG))
            return jnp.max(functools.reduce(jnp.maximum, parts))

        def finish(state):
            out_t = jnp.concatenate([acc / l for (l, acc) in state], axis=0)
            return out_t.T

        def walk(q_past, step_fn, state, pair_groups):
            def scores(blk):
                start = pl.multiple_of(blk * bs, bs)
                return tuple(_dot_nt(kaug_ref[h, pl.ds(start, bs), :], q_past[h]) for h in heads)

            def visit(first_blk, n_blk, state):
                state = list(state)
                s_next = scores(first_blk) if n_blk else None
                for c in range(n_blk):
                    s_cur, s_next = s_next, (scores(first_blk + c + 1) if c + 1 < n_blk else None)
                    for h in heads:
                        state[h] = step_fn(state[h], s_cur[h], vt_ref[first_blk + c, h * hd:(h + 1) * hd, :])
                return tuple(state)

            if not pair_groups:
                return lax.fori_loop(0, (i + grp - 1) // grp, lambda g, st: visit(g * grp, grp, st), tuple(state))
            whole = i // grp
            state = lax.fori_loop(0, whole // 2, lambda g, st: visit(g * 2 * grp, 2 * grp, st), tuple(state))
            state = lax.fori_loop(whole // 2 * 2, whole, lambda g, st: visit(g * grp, grp, st), state)
            rest = [functools.partial(visit, whole * grp, r) for r in range(grp)]
            return lax.switch(i % grp, rest, state)

        def shifted():
            sel = select()
            q_past = [_bf16(jnp.where(lane < hd, dims[h], jnp.where(sel[h], -shift[h], NEG_BIG))) for h in heads]
            own = [_dot_nt(kaug_ref[h, pl.ds(own_start, bs), :],
                           _bf16(jnp.where(lane < hd, dims[h], jnp.where(lane == hd + i, -shift[h], 0.0))))
                   for h in heads]
            state = []
            for h in heads:
                p = jnp.where(causal, jnp.exp(own[h]), 0.0)
                state.append((jnp.sum(p, axis=0, keepdims=True), _dot(vt_own[h * hd:(h + 1) * hd, :], _bf16(p))))
            sb_state = tuple(_sb_weights(vs_ref, sb_jobs, sb_logs))
            sb_worst = sb_bound(sb_state, 0)

            def step_fn(st, s, vt):
                p = jnp.exp(s)
                return st[0] + jnp.sum(p, axis=0, keepdims=True), st[1] + _dot(vt, _bf16(p))

            return finish(walk(q_past, step_fn, state, True)), sb_state, sb_worst

        def running_max():
            sb_state = tuple(_sb_weights(vs_ref, sb_jobs, sb_logs))
            sb_worst = sb_bound(sb_state, 0)
            sel = select()
            q_past, state = [], []
            for h in heads:
                q_past.append(_bf16(jnp.where(lane < hd, dims[h], jnp.where(sel[h], 0.0, NEG_BIG))))
                q_own = _bf16(jnp.where(lane < hd, dims[h], 0.0))
                s = jnp.where(causal, _dot_nt(kaug_ref[h, pl.ds(own_start, bs), :], q_own), NEG_BIG)
                m = jnp.max(s, axis=0, keepdims=True)
                p = jnp.exp(s - m)
                state.append((m, jnp.sum(p, axis=0, keepdims=True), _dot(vt_own[h * hd:(h + 1) * hd, :], _bf16(p))))

            def step_fn(st, s, vt):
                m_old, l_old, acc_old = st
                m_new = jnp.maximum(m_old, jnp.max(s, axis=0, keepdims=True))
                p = jnp.exp(s - m_new)
                alpha = jnp.exp(m_old - m_new)
                return m_new, alpha * l_old + jnp.sum(p, axis=0, keepdims=True), alpha * acc_old + _dot(vt, _bf16(p))

            return finish([st[1:] for st in walk(q_past, step_fn, state, False)]), sb_state, sb_worst

        largest = jnp.max(functools.reduce(jnp.maximum, shift))
        sb_logs = _sb_logs(tri_ref, sb_jobs, _sb_scores(ks_ref, sb_jobs))
        moba_out, sb_state, sb_worst = lax.cond(2.0 * largest <= MOBA_SHIFT_MAX, shifted, running_max)
        o_ref[r0:r0 + bs, :] = _bf16(moba_out)

        def cond(loop):
            return loop[1] > SB_UNDERFLOW

        def body(loop):
            k, _, state = loop
            jobs = []
            for t in range(n_tiles):
                left = sb_starts[t] - k * step
                start = pl.multiple_of(jnp.maximum(left - step, 0), step)
                live = jnp.broadcast_to(left > 0, (tq, step))
                for u in range(per_tile):
                    job = t * per_tile + u
                    jobs.append((sb_q[job], sb_jobs[job][1], start, step, state[job], live))
            state = tuple(_sb_weights(vs_ref, jobs, _sb_logs(tri_ref, jobs, _sb_scores(ks_ref, jobs))))
            return k + 1, sb_bound(state, k + 1), state

        _, _, sb_state = lax.while_loop(cond, body, (0, sb_worst, sb_state))
        for t in range(n_tiles):
            for pr in range(ATTN_PAIRS):
                job = t * per_tile + pr * HEADS_PER_TILE
                os_ref[r0 + t * tq:r0 + (t + 1) * tq, pr * LANES:(pr + 1) * LANES] = _bf16(
                    jnp.where(sb_lane < hd, sb_state[job][1], sb_state[job + 1][1]))

    for qb in range(ATTN_QBLOCKS):
        block(step_id * ATTN_QBLOCKS + qb, qb * bs)


def _attention(qkv, batch, seq):
    t = qkv.shape[0]
    n_blocks = seq // MOBA_BLOCK
    assert n_blocks % MOBA_GROUP == 0, "the grouped block walk reads whole groups"
    assert n_blocks <= MOBA_TAGS, "one block tag lane per key block"
    assert MOBA_BLOCK % SB_TQ == 0 and seq >= SB_WINDOW and SB_WIDTH == MOBA_WIDTH
    assert n_blocks % ATTN_QBLOCKS == 0
    nq = n_blocks // ATTN_QBLOCKS
    step_rows = ATTN_QBLOCKS * MOBA_BLOCK
    width = ATTN_PAIRS * LANES
    hp = MOBA_WIDTH // width
    n_heads = ATTN_PAIRS * HEADS_PER_TILE
    idx = jnp.arange(SB_WINDOW)
    tri = _bf16(idx[:, None] > idx[None, :])
    tri2 = jnp.concatenate([tri, tri], axis=0)
    rows_blk = lambda sec: pl.BlockSpec((step_rows, width), lambda b, p, i: (b * nq + i, sec * hp + p))
    seq_blk = lambda sec, bufs: pl.BlockSpec((seq, width), lambda b, p, i: (b, sec * hp + p),
                                             pipeline_mode=pl.Buffered(bufs))
    out_blk = pl.BlockSpec((step_rows, width), lambda b, p, i: (b * nq + i, p))
    return pl.pallas_call(
        functools.partial(_attn_kernel, n_blocks=n_blocks),
        grid=(batch, hp, nq),
        in_specs=[rows_blk(0), seq_blk(1, 2), seq_blk(2, 2),
                  pl.BlockSpec((2 * SB_WINDOW, SB_WINDOW), lambda b, p, i: (0, 0)),
                  rows_blk(3), seq_blk(4, 1), seq_blk(5, 1)],
        out_specs=[out_blk, out_blk],
        out_shape=[jax.ShapeDtypeStruct((t, SB_WIDTH), jnp.bfloat16),
                   jax.ShapeDtypeStruct((t, MOBA_WIDTH), jnp.bfloat16)],
        scratch_shapes=[pltpu.VMEM((ATTN_PAIRS, LANES, LANES), jnp.float32),
                        pltpu.VMEM((n_heads, 1, LANES), jnp.float32),
                        pltpu.VMEM((n_heads, seq, LANES), jnp.bfloat16),
                        pltpu.VMEM((n_blocks, width, MOBA_BLOCK), jnp.bfloat16)],
        compiler_params=_params(3),
        name="attention",
    )(qkv, qkv, qkv, tri2, qkv, qkv, qkv)


def _mix_kernel(x_ref, osb_ref, omb_ref, wgs_ref, wgm_ref, wbs_ref, wbm_ref, wo_ref, g_ref, b_ref, o_ref, *, alpha):
    wgs, wgm, wbs, wbm, wo = (_bf16(r[...]) for r in (wgs_ref, wgm_ref, wbs_ref, wbm_ref, wo_ref))
    rows = x_ref.shape[0] // MIX_CHUNKS
    for c in range(MIX_CHUNKS):
        r = slice(c * rows, (c + 1) * rows)
        x = x_ref[r, :]
        xb = _bf16(x)
        branch_sb = _dot(osb_ref[r, :], wbs)
        branch_mb = _dot(omb_ref[r, :], wbm)
        merged = (jax.nn.sigmoid(_dot(xb, wgs)) * branch_sb
                  + jax.nn.sigmoid(_dot(xb, wgm)) * branch_mb)
        mix = _dot(_bf16(merged), wo)
        o_ref[r, :] = _layer_norm(alpha * x + mix, g_ref[...], b_ref[...])


def _mix_block(x2, o_sb, o_mb, w_in, w_bsb, w_bmb, w_out, ln_g, ln_b, l, alpha, tm):
    t, d = x2.shape
    assert QKV_COLS % d == 0
    rows = lambda w: pl.BlockSpec((tm, w), lambda i: (i, 0))
    whole = lambda a: _layer_block(l, a.shape[1:])
    gate_cols = lambda j: _layer_block(l, (d, d), QKV_COLS // d + j)
    return pl.pallas_call(
        functools.partial(_mix_kernel, alpha=alpha),
        grid=(t // tm,),
        in_specs=[rows(d), rows(SB_WIDTH), rows(MOBA_WIDTH),
                  gate_cols(0), gate_cols(1), whole(w_bsb), whole(w_bmb), whole(w_out), whole(ln_g), whole(ln_b)],
        out_specs=rows(d),
        out_shape=jax.ShapeDtypeStruct((t, d), jnp.float32),
        compiler_params=_params(1),
        name="mix_ln",
    )(x2, o_sb, o_mb, w_in, w_in, w_bsb, w_bmb, w_out, ln_g, ln_b)


def _ffn_kernel(x_ref, wg_ref, wu_ref, wd_ref, g_ref, b_ref, o_ref, *, alpha, chunk):
    x = x_ref[...]
    xb = _bf16(x)
    acc = alpha * x
    for c in range(0, wg_ref.shape[1], chunk):
        gate = _dot(xb, wg_ref[:, c:c + chunk])
        up = _dot(xb, wu_ref[:, c:c + chunk])
        acc = acc + _dot(_bf16(jax.nn.silu(gate) * up), wd_ref[c:c + chunk, :])
    o_ref[...] = _layer_norm(acc, g_ref[...], b_ref[...])


def _ffn_chunk(d_ff):
    for c in range(MXU_WIDTH, d_ff, MXU_WIDTH):
        if d_ff % c == 0:
            return c
    return d_ff


def _ffn_block(x2, w_gate, w_up, w_down, ln_g, ln_b, l, alpha, tm):
    t, d = x2.shape
    rows = pl.BlockSpec((tm, d), lambda i: (i, 0))
    whole = lambda a: _layer_block(l, a.shape[1:])
    return pl.pallas_call(
        functools.partial(_ffn_kernel, alpha=alpha, chunk=_ffn_chunk(w_gate.shape[2])),
        grid=(t // tm,),
        in_specs=[rows, whole(w_gate), whole(w_up), whole(w_down), whole(ln_g), whole(ln_b)],
        out_specs=rows,
        out_shape=jax.ShapeDtypeStruct((t, d), jnp.float32),
        compiler_params=_params(1),
        name="ffn_ln",
    )(x2, w_gate, w_up, w_down, ln_g, ln_b)


def _rope_tables(seq):
    half = ROPE_DIM // 2
    inv_freq = ROPE_THETA ** (-jnp.arange(0, ROPE_DIM, 2, dtype=jnp.float32) / ROPE_DIM)
    dim = np.arange(LANES) % HEAD_DIM
    ang = jnp.arange(seq).astype(jnp.float32)[:, None] * inv_freq[dim % half][None, :]
    cos, sin = jnp.cos(ang), jnp.sin(ang)
    rotary = dim < ROPE_DIM
    return (jnp.where(rotary, cos, 1.0),
            jnp.where(rotary & (dim >= half), sin, 0.0),
            jnp.where(dim < half, -sin, 0.0))


def kernel(x, w_in, w_branch_sb, w_branch_moba, w_out, ln_mix_g, ln_mix_b,
           w_ffn_gate, w_ffn_up, w_ffn_down, ln_ffn_g, ln_ffn_b):
    batch, seq, d = x.shape
    depth = w_in.shape[0]
    assert w_in.shape[2] == QKV_COLS + 2 * d
    assert seq % MOBA_BLOCK == 0
    alpha = (2 * depth) ** 0.25
    tm = min(1024, seq)
    cos_t, sina_t, sinb_t = _rope_tables(seq)
    x2 = x.reshape(batch * seq, d)
    w_bsb, w_bmb, w_o = w_branch_sb, w_branch_moba, w_out
    w_fg, w_fu, w_fd = _bf16(w_ffn_gate), _bf16(w_ffn_up), _bf16(w_ffn_down)
    row = lambda a: a[:, None, :]
    for l in range(depth):
        qkv = _qkv_proj(x2, w_in, l, cos_t, sina_t, sinb_t, seq, tm)
        o_sb, o_mb = _attention(qkv, batch, seq)
        x2 = _mix_block(x2, o_sb, o_mb, w_in, w_bsb, w_bmb, w_o, row(ln_mix_g), row(ln_mix_b), l, alpha, tm)
        x2 = _ffn_block(x2, w_fg, w_fu, w_fd, row(ln_ffn_g), row(ln_ffn_b), l, alpha, tm)
    return x2.reshape(batch, seq, d)
```
